```python
import jax, jax.numpy as jnp
from jax import lax
import numpy as np


D_MODEL = 2048
BATCH = 2
SEQ = 4096
DEPTH = 1
DEC_BATCH = 4
DEC_SEQ = 4096
PAST_LEN = 128

HG_DK = 128
HG_DV = 128
HG_HEADS = D_MODEL // HG_DK
HG_FDIM = HG_HEADS * HG_DK
HG_WIDTH = HG_HEADS * HG_DV
CHUNK = 64
CONV_WIDTH = D_MODEL // 2
CONV_K = 31
N_EXPERTS = 32
TOP_K = 4
D_FF = D_MODEL
SWIGLU_ALPHA = 1.702
SWIGLU_LIMIT = 7.0
PLE_DIM = 256
DN_ALPHA = (2.0 * DEPTH) ** 0.25
DN_BETA = (8.0 * DEPTH) ** -0.25
NORM_EPS = 1e-5
IN_SIZES = (HG_FDIM, HG_FDIM, HG_FDIM, HG_WIDTH, HG_WIDTH, 2 * CONV_WIDTH, D_MODEL, D_MODEL)
IN_DIM = sum(IN_SIZES)
SPLIT_POINTS = tuple(int(v) for v in np.cumsum(IN_SIZES)[:-1])

kernel_name = 'hgrn2_conformer_moe_encoder'


def _layer_norm(x, g, b):
    xf = x.astype(jnp.float32)
    mu = jnp.mean(xf, axis=-1, keepdims=True)
    xc = xf - mu
    var = jnp.mean(xc * xc, axis=-1, keepdims=True)
    return (xc * lax.rsqrt(var + NORM_EPS) * g + b).astype(x.dtype)


def _rms_norm(x, g):
    xf = x.astype(jnp.float32)
    return (xf * lax.rsqrt(jnp.mean(xf * xf, axis=-1, keepdims=True) + NORM_EPS) * g).astype(x.dtype)


def _chunk_scan(q, k, v, logf):
    B, S = q.shape[0], q.shape[1]
    nc = S // CHUNK

    def to_chunks(t):
        return t.reshape(B, nc, CHUNK, HG_HEADS, t.shape[-1]).transpose(1, 0, 3, 2, 4)

    qc, kc, vc, gc = (to_chunks(t) for t in (q, k, v, logf))
    mask = jnp.tril(jnp.ones((CHUNK, CHUNK), dtype=bool))[:, :, None]

    def step(state, inp):
        qi, ki, vi, gi = inp
        b = jnp.cumsum(gi, axis=2)
        b_last = b[:, :, -1:, :]
        diff = b[:, :, :, None, :] - b[:, :, None, :, :]
        decay = jnp.exp(jnp.where(mask, diff, -jnp.inf))
        scores = jnp.einsum('bhtk,bhtsk,bhsk->bhts', qi, decay, ki)
        o_intra = jnp.einsum('bhts,bhsv->bhtv', scores, vi)
        o_inter = jnp.einsum('bhtk,bhkv->bhtv', qi * jnp.exp(b), state)
        k_dec = ki * jnp.exp(b_last - b)
        new_state = jnp.exp(b_last[:, :, 0, :])[..., None] * state + jnp.einsum('bhsk,bhsv->bhkv', k_dec, vi)
        return new_state, o_intra + o_inter

    s0 = jnp.zeros((B, HG_HEADS, HG_DK, HG_DV), jnp.float32)
    _, o = lax.scan(step, s0, (qc, kc, vc, gc))
    return o.transpose(1, 0, 3, 2, 4).reshape(B, S, HG_HEADS, HG_DV)


def _hgrn2_branch(q, f_fwd, f_bwd, v, og, lb, norm_g, w_o):
    B, S, _ = q.shape

    def heads(t):
        return t.astype(jnp.float32).reshape(B, S, HG_HEADS, -1)

    qh = heads(jax.nn.silu(q))
    vh = heads(v)

    def one_direction(f_logits, lb_d, reverse):
        gate = lb_d + (1.0 - lb_d) * jax.nn.sigmoid(f_logits.astype(jnp.float32))
        kh = heads(1.0 - gate)
        gh = heads(jnp.log(gate))
        if reverse:
            o = _chunk_scan(*(jnp.flip(t, axis=1) for t in (qh, kh, vh, gh)))
            return jnp.flip(o, axis=1)
        return _chunk_scan(qh, kh, vh, gh)

    o = one_direction(f_fwd, lb[0], False) + one_direction(f_bwd, lb[1], True)
    o = _rms_norm(o.reshape(B, S, HG_WIDTH), norm_g) * jax.nn.silu(og.astype(jnp.float32))
    return o.astype(q.dtype) @ w_o


def _conformer_conv(u, dw_w, dw_b, ln_g, ln_b, w_pw):
    a, g = jnp.split(u, 2, axis=-1)
    h = a * jax.nn.sigmoid(g)
    h = lax.conv_general_dilated(
        h, dw_w[:, None, :], window_strides=(1,),
        padding=[(CONV_K // 2, CONV_K // 2)],
        dimension_numbers=('NWC', 'WIO', 'NWC'),
        feature_group_count=CONV_WIDTH) + dw_b
    h = jax.nn.silu(_layer_norm(h, ln_g, ln_b))
    return h @ w_pw


def _moe(x, w_r, b_r, w1, b1, w2, b2):
    shape = x.shape
    t = x.reshape(-1, shape[-1])
    logits = (t @ w_r + b_r).astype(jnp.float32)
    top_v, top_i = lax.top_k(logits, TOP_K)
    top_w = jax.nn.softmax(top_v, axis=-1)
    gates = jnp.einsum('tk,tke->te', top_w, jax.nn.one_hot(top_i, N_EXPERTS, dtype=jnp.float32))
    out = jnp.zeros(t.shape, jnp.float32)
    for e in range(N_EXPERTS):
        h = t @ w1[e] + b1[e]
        hg, hl = jnp.split(h, 2, axis=-1)
        hg = jnp.minimum(hg, SWIGLU_LIMIT)
        hl = jnp.clip(hl, -SWIGLU_LIMIT, SWIGLU_LIMIT)
        act = hg * jax.nn.sigmoid(SWIGLU_ALPHA * hg) * (hl + 1.0)
        out = out + gates[:, e:e + 1] * (act @ w2[e] + b2[e])
    return out.astype(x.dtype).reshape(shape)


def _encoder_layer(x, p_l, lb, w_in, hg_norm_g, w_hg_out, conv_w, conv_b, conv_ln_g, conv_ln_b,
                   w_conv_out, w_out, ln1_g, ln1_b, w_router, b_router, w_exp1, b_exp1,
                   w_exp2, b_exp2, w_ple_gate, w_ple_proj, ln2_g, ln2_b):
    proj = x @ w_in
    q, f_f, f_b, iv, og, glu, ga, gc = jnp.split(proj, SPLIT_POINTS, axis=-1)
    a = _hgrn2_branch(q, f_f, f_b, iv, og, lb, hg_norm_g, w_hg_out)
    c = _conformer_conv(glu, conv_w, conv_b, conv_ln_g, conv_ln_b, w_conv_out)
    mixed = (jax.nn.sigmoid(ga) * a + jax.nn.sigmoid(gc) * c) @ w_out
    x = _layer_norm(DN_ALPHA * x + mixed, ln1_g, ln1_b)
    ple = jax.nn.sigmoid(x @ w_ple_gate) * (p_l @ w_ple_proj)
    x = _layer_norm(DN_ALPHA * x + _moe(x, w_router, b_router, w_exp1, b_exp1, w_exp2, b_exp2) + ple,
                    ln2_g, ln2_b)
    return x


def _trunk(x, p, ln_emb_g, ln_emb_b, w_in, lower_bounds, hg_norm_g, w_hg_out, conv_w, conv_b,
           conv_ln_g, conv_ln_b, w_conv_out, w_out, ln1_g, ln1_b, w_router, b_router, w_exp1,
           b_exp1, w_exp2, b_exp2, w_ple_gate, w_ple_proj, ln2_g, ln2_b):
    lbs = jnp.cumsum(jax.nn.softmax(lower_bounds.astype(jnp.float32), axis=1), axis=1)
    h = _layer_norm(x, ln_emb_g, ln_emb_b)
    for l in range(DEPTH):
        h = _encoder_layer(h, p[l], lbs[:, l], w_in[l], hg_norm_g[l], w_hg_out[l], conv_w[l], conv_b[l],
                           conv_ln_g[l], conv_ln_b[l], w_conv_out[l], w_out[l], ln1_g[l], ln1_b[l],
                           w_router[l], b_router[l], w_exp1[l], b_exp1[l], w_exp2[l], b_exp2[l],
                           w_ple_gate[l], w_ple_proj[l], ln2_g[l], ln2_b[l])
    return h


def setup_inputs(seed: int = 0) -> dict:
    key = jax.random.key(seed)
    ks = jax.random.split(key, 32)

    def n(k, s):
        return jax.random.normal(k, s, jnp.float32)

    return {
        'x_prompt': n(ks[0], (BATCH, SEQ, D_MODEL)),
        'x_sample': n(ks[1], (DEC_BATCH, DEC_SEQ, D_MODEL)),
        'p_prompt': n(ks[2], (DEPTH, BATCH, SEQ, PLE_DIM)),
        'p_sample': n(ks[3], (DEPTH, DEC_BATCH, DEC_SEQ, PLE_DIM)),
        'ln_emb_g': 1.0 + 0.02 * n(ks[4], (D_MODEL,)),
        'ln_emb_b': 0.02 * n(ks[5], (D_MODEL,)),
        'w_in': n(ks[6], (DEPTH, D_MODEL, IN_DIM)) * D_MODEL ** -0.5,
        'lower_bounds': 0.1 * n(ks[7], (2, DEPTH + 1, HG_FDIM)),
        'hg_norm_g': 1.0 + 0.02 * n(ks[8], (DEPTH, HG_WIDTH)),
        'w_hg_out': n(ks[9], (DEPTH, HG_WIDTH, D_MODEL)) * (DN_BETA * HG_WIDTH ** -0.5),
        'conv_w': n(ks[10], (DEPTH, CONV_K, CONV_WIDTH)) * CONV_K ** -0.5,
        'conv_b': 0.02 * n(ks[11], (DEPTH, CONV_WIDTH)),
        'conv_ln_g': 1.0 + 0.02 * n(ks[12], (DEPTH, CONV_WIDTH)),
        'conv_ln_b': 0.02 * n(ks[13], (DEPTH, CONV_WIDTH)),
        'w_conv_out': n(ks[14], (DEPTH, CONV_WIDTH, D_MODEL)) * (DN_BETA * CONV_WIDTH ** -0.5),
        'w_out': n(ks[15], (DEPTH, D_MODEL, D_MODEL)) * (DN_BETA * D_MODEL ** -0.5),
        'ln1_g': 1.0 + 0.02 * n(ks[16], (DEPTH, D_MODEL)),
        'ln1_b': 0.02 * n(ks[17], (DEPTH, D_MODEL)),
        'w_router': n(ks[18], (DEPTH, D_MODEL, N_EXPERTS)) * D_MODEL ** -0.5,
        'b_router': 0.01 * n(ks[19], (DEPTH, N_EXPERTS)),
        'w_exp1': n(ks[20], (DEPTH, N_EXPERTS, D_MODEL, 2 * D_FF)) * D_MODEL ** -0.5,
        'b_exp1': 0.02 * n(ks[21], (DEPTH, N_EXPERTS, 2 * D_FF)),
        'w_exp2': n(ks[22], (DEPTH, N_EXPERTS, D_FF, D_MODEL)) * (DN_BETA * D_FF ** -0.5),
        'b_exp2': 0.02 * n(ks[23], (DEPTH, N_EXPERTS, D_MODEL)),
        'w_ple_gate': n(ks[24], (DEPTH, D_MODEL, D_MODEL)) * D_MODEL ** -0.5,
        'w_ple_proj': n(ks[25], (DEPTH, PLE_DIM, D_MODEL)) * (DN_BETA * PLE_DIM ** -0.5),
        'ln2_g': 1.0 + 0.02 * n(ks[26], (DEPTH, D_MODEL)),
        'ln2_b': 0.02 * n(ks[27], (DEPTH, D_MODEL)),
    }


def reference(x_prompt, x_sample, p_prompt, p_sample, ln_emb_g, ln_emb_b, w_in, lower_bounds,
              hg_norm_g, w_hg_out, conv_w, conv_b, conv_ln_g, conv_ln_b, w_conv_out, w_out,
              ln1_g, ln1_b, w_router, b_router, w_exp1, b_exp1, w_exp2, b_exp2,
              w_ple_gate, w_ple_proj, ln2_g, ln2_b):
    shared = (ln_emb_g, ln_emb_b, w_in, lower_bounds, hg_norm_g, w_hg_out, conv_w, conv_b,
              conv_ln_g, conv_ln_b, w_conv_out, w_out, ln1_g, ln1_b, w_router, b_router,
              w_exp1, b_exp1, w_exp2, b_exp2, w_ple_gate, w_ple_proj, ln2_g, ln2_b)
    y_prompt = _trunk(x_prompt, p_prompt, *shared)
    y_sample = _trunk(x_sample, p_sample, *shared)
    return (y_prompt, y_sample)
```

```python
import functools

import jax
import jax.numpy as jnp
from jax import lax
from jax.experimental import pallas as pl
from jax.experimental.pallas import tpu as pltpu

F32 = jnp.float32
BF16 = jnp.bfloat16
U32 = jnp.uint32
I32 = jnp.int32

LANES = 128
NORM_EPS = 1e-5
CONV_K = 31
TOP_K = 4
SWIGLU_ALPHA = 1.702
SWIGLU_LIMIT = 7.0
SCAN_CHUNK = 64
DIAG = 8
VMEM_LIMIT = 56 * 1024 * 1024


def _cparams(sem, vmem=VMEM_LIMIT):
    return pltpu.CompilerParams(dimension_semantics=sem, vmem_limit_bytes=vmem)


def _resident(shape):
    nd = len(shape)
    return pl.BlockSpec(shape, lambda *_: (0,) * nd, pipeline_mode=pl.Buffered(1))


def _ln(x, g, b):
    mu = jnp.mean(x, axis=-1, keepdims=True)
    xc = x - mu
    var = jnp.mean(xc * xc, axis=-1, keepdims=True)
    return xc * lax.rsqrt(var + NORM_EPS) * g + b


def _sigmoid(x):
    return 1.0 / (1.0 + jnp.exp(-x))


def _pack_bf16_pair(hi, lo):
    hb = pltpu.bitcast(hi.astype(BF16).astype(F32), U32)
    lb = pltpu.bitcast(lo.astype(BF16).astype(F32), U32)
    return (hb & jnp.uint32(0xFFFF0000)) | (lb >> 16)


def _unpack_bf16_pair(w):
    hi = pltpu.bitcast(w & jnp.uint32(0xFFFF0000), F32)
    lo = pltpu.bitcast(w << 16, F32)
    return hi, lo


def _inproj_kernel(x_ref, g_ref, b_ref, w_ref, o_ref, h_ref):
    @pl.when(pl.program_id(1) == 0)
    def _():
        h_ref[...] = _ln(x_ref[...], g_ref[...], b_ref[...]).astype(BF16)

    acc = jnp.dot(h_ref[...], w_ref[...], preferred_element_type=F32)
    for c in range(o_ref.shape[0]):
        o_ref[c] = acc[:, c * LANES:(c + 1) * LANES].astype(BF16)


def _inproj(x, g, b, w_bf, tm, tn):
    T, D = x.shape
    N = w_bf.shape[1]
    return pl.pallas_call(
        _inproj_kernel,
        grid=(T // tm, N // tn),
        in_specs=[
            pl.BlockSpec((tm, D), lambda i, j: (i, 0)),
            pl.BlockSpec((1, D), lambda i, j: (0, 0)),
            pl.BlockSpec((1, D), lambda i, j: (0, 0)),
            pl.BlockSpec((D, tn), lambda i, j: (0, j)),
        ],
        out_specs=pl.BlockSpec((tn // LANES, tm, LANES), lambda i, j: (j, i, 0)),
        out_shape=jax.ShapeDtypeStruct((N // LANES, T, LANES), BF16),
        scratch_shapes=[pltpu.VMEM((tm, D), BF16)],
        compiler_params=_cparams(("parallel", "arbitrary")),
        name="inproj",
    )(x, g, b, w_bf)


def _cumsum_rows(g, tri):
    g1 = g.astype(BF16)
    r1 = g - g1.astype(F32)
    g2 = r1.astype(BF16)
    g3 = (r1 - g2.astype(F32)).astype(BF16)
    out = jnp.dot(tri, g1, preferred_element_type=F32)
    out = out + jnp.dot(tri, g2, preferred_element_type=F32)
    return out + jnp.dot(tri, g3, preferred_element_type=F32)


def _scan_chunk(q, v_bf, vt_bf, fl, lb, st, rev):
    C = q.shape[0]
    gate = lb + (1.0 - lb) * _sigmoid(fl)
    k = 1.0 - gate
    g = jnp.log(gate)
    row = lax.broadcasted_iota(I32, (C, C), 0)
    col = lax.broadcasted_iota(I32, (C, C), 1)
    tri = (col >= row) if rev else (col <= row)
    b = _cumsum_rows(g, jnp.where(tri, 1.0, 0.0).astype(BF16))

    scores = jnp.zeros((C, C), F32)
    m = C // 2
    while m >= DIAG:
        n = C // m
        bm = b.reshape(n, m, b.shape[-1])
        zero = jnp.zeros((1, 1, b.shape[-1]), F32)
        if rev:
            edge = bm[:, 0:1, :]
            ref_q = jnp.concatenate([edge[1:], zero], axis=0)
        else:
            edge = bm[:, m - 1:m, :]
            ref_q = jnp.concatenate([zero, edge[:-1]], axis=0)
        eq = (bm - ref_q).reshape(C, -1)
        ek = (edge - bm).reshape(C, -1)
        ql = (q * jnp.exp(eq)).astype(BF16)
        kl = (k * jnp.exp(ek)).astype(BF16)
        s_l = lax.dot_general(ql, kl, (((1,), (1,)), ((), ())), preferred_element_type=F32)
        rb, cb = row // m, col // m
        if rev:
            mask = (cb == rb + 1) & ((rb & 1) == 0)
        else:
            mask = (rb == cb + 1) & ((cb & 1) == 0)
        scores = scores + jnp.where(mask, s_l, 0.0)
        m //= 2

    rin = row & (DIAG - 1)
    gprod = None
    for d in range(DIAG):
        if d == 0:
            a = q * k
            mask = row == col
        else:
            sh = (C - d) if rev else d
            gprod = gate if d == 1 else gate * pltpu.roll(gprod, (C - 1) if rev else 1, axis=0)
            a = q * pltpu.roll(k, sh, axis=0) * gprod
            if rev:
                mask = (col == row + d) & (rin + d < DIAG)
            else:
                mask = (col == row - d) & (rin >= d)
        scores = scores + jnp.where(mask, jnp.sum(a, axis=-1, keepdims=True), 0.0)

    o = jnp.dot(scores.astype(BF16), v_bf, preferred_element_type=F32)
    qe = (q * jnp.exp(b)).astype(BF16)
    o = o + lax.dot_general(qe, st.astype(BF16), (((1,), (1,)), ((), ())), preferred_element_type=F32)
    b_end = b[0:1, :] if rev else b[C - 1:C, :]
    kdec = (k * jnp.exp(b_end - b)).astype(BF16)
    st_new = st * jnp.exp(b_end) + jnp.dot(vt_bf, kdec, preferred_element_type=F32)
    return o, st_new


def _scan_kernel(qf_ref, vf_ref, ff_ref, qb_ref, vb_ref, fb_ref, lbw_ref, of_ref, ob_ref, sf_ref, sb_ref,
                 *, layer):
    @pl.when(pl.program_id(1) == 0)
    def _():
        sf_ref[...] = jnp.zeros_like(sf_ref)
        sb_ref[...] = jnp.zeros_like(sb_ref)

    def lower_bound(d, h):
        raw = lbw_ref[d, :, h]
        e = jnp.exp(raw - jnp.max(raw, axis=0, keepdims=True))
        return jnp.sum(e[:layer + 1], axis=0) / jnp.sum(e, axis=0)

    def body(h, carry):
        for rev, (q_ref, v_ref, f_ref, o_ref, s_ref) in enumerate(
                ((qf_ref, vf_ref, ff_ref, of_ref, sf_ref), (qb_ref, vb_ref, fb_ref, ob_ref, sb_ref))):
            qr = q_ref[h].astype(F32)
            q = qr * _sigmoid(qr)
            v = v_ref[h].astype(F32)
            o, st = _scan_chunk(q, v.astype(BF16), v.T.astype(BF16), f_ref[h].astype(F32),
                                lower_bound(rev, h), s_ref[h], bool(rev))
            o_ref[h] = o.astype(BF16)
            s_ref[h] = st
        return carry

    lax.fori_loop(0, qf_ref.shape[0], body, 0)


def _scan(P, lbw, n_seq, seq, layer):
    H = lbw.shape[2]
    T = P.shape[1]
    C = SCAN_CHUNK
    nc = seq // C

    def spec(sec, rev):
        if rev:
            return pl.BlockSpec((H, C, LANES), lambda s, c: (sec, s * nc + nc - 1 - c, 0))
        return pl.BlockSpec((H, C, LANES), lambda s, c: (sec, s * nc + c, 0))

    out_sds = jax.ShapeDtypeStruct((H, T, LANES), BF16)
    return pl.pallas_call(
        functools.partial(_scan_kernel, layer=layer),
        grid=(n_seq, nc),
        in_specs=[spec(0, False), spec(3, False), spec(1, False),
                  spec(0, True), spec(3, True), spec(2, True),
                  pl.BlockSpec(lbw.shape, lambda s, c: (0,) * lbw.ndim)],
        out_specs=[spec(0, False), spec(0, True)],
        out_shape=[out_sds, out_sds],
        scratch_shapes=[pltpu.VMEM((H, LANES, LANES), F32), pltpu.VMEM((H, LANES, LANES), F32)],
        compiler_params=_cparams(("parallel", "arbitrary")),
        name="scan",
    )(P, P, P, P, P, P, lbw)


def _mix_kernel(of_ref, ob_ref, og_ref, glu_ref, gprev_ref, gnext_ref, ga_ref, gc_ref, x_ref,
                whg_ref, wcv_ref, wout_ref, hgn_ref, cw_ref, cb_ref, clg_ref, clb_ref,
                eg_ref, eb_ref, l1g_ref, l1b_ref, x1_ref, hc_ref, cv_ref, *, tiles_per_seq, alpha):
    H, tm, _ = of_ref.shape
    nch = glu_ref.shape[0] // 2
    halo = gprev_ref.shape[1]
    i = pl.program_id(0)
    first = (i % tiles_per_seq) == 0
    last = (i % tiles_per_seq) == tiles_per_seq - 1

    def cat(ref, lo=0, hi=None):
        hi = ref.shape[0] if hi is None else hi
        return jnp.concatenate([ref[c].astype(F32) for c in range(lo, hi)], axis=-1)

    o = cat(of_ref) + cat(ob_ref)
    og = cat(og_ref)
    on = o * lax.rsqrt(jnp.mean(o * o, axis=-1, keepdims=True) + NORM_EPS) * hgn_ref[...]
    a = jnp.dot((on * (og * _sigmoid(og))).astype(BF16), whg_ref[...], preferred_element_type=F32)

    def glu(ref):
        return cat(ref, 0, nch) * _sigmoid(cat(ref, nch, 2 * nch))

    hc_ref[pl.ds(0, halo), :] = jnp.where(first, 0.0, glu(gprev_ref))
    hc_ref[pl.ds(halo, tm), :] = glu(glu_ref)
    hc_ref[pl.ds(halo + tm, halo), :] = jnp.where(last, 0.0, glu(gnext_ref))
    rb = 32
    for c in range(nch):
        cs = slice(c * LANES, (c + 1) * LANES)
        taps = [cw_ref[j:j + 1, cs] for j in range(CONV_K)]
        for r0 in range(0, tm, rb):
            acc = jnp.zeros((rb, LANES), F32)
            for j in range(CONV_K):
                lo = r0 + halo - CONV_K // 2 + j
                acc = acc + taps[j] * hc_ref[lo:lo + rb, cs]
            cv_ref[r0:r0 + rb, cs] = acc
    cn = _ln(cv_ref[...] + cb_ref[...], clg_ref[...], clb_ref[...])
    cc = jnp.dot((cn * _sigmoid(cn)).astype(BF16), wcv_ref[...], preferred_element_type=F32)

    mixed = jnp.dot((_sigmoid(cat(ga_ref)) * a + _sigmoid(cat(gc_ref)) * cc).astype(BF16), wout_ref[...],
                    preferred_element_type=F32)
    h = _ln(x_ref[...], eg_ref[...], eb_ref[...])
    x1_ref[...] = _ln(alpha * h + mixed, l1g_ref[...], l1b_ref[...])


def _mix(o_f, o_b, P, x, whg, wcv, wout, small, seq, tm, alpha):
    H, T, _ = o_f.shape
    D = x.shape[1]
    Cw = wcv.shape[0]
    halo = 16
    nt = T // tm
    hb = tm // halo
    last_hb = T // halo - 1
    sec = lambda s: pl.BlockSpec((H, tm, LANES), lambda i: (s, i, 0))
    vec = lambda n: pl.BlockSpec((1, n), lambda i: (0, 0))
    in_specs = [
        sec(0), sec(0), sec(4), sec(5),
        pl.BlockSpec((H, halo, LANES), lambda i: (5, jnp.maximum(i * hb - 1, 0), 0)),
        pl.BlockSpec((H, halo, LANES), lambda i: (5, jnp.minimum((i + 1) * hb, last_hb), 0)),
        sec(6), sec(7),
        pl.BlockSpec((tm, D), lambda i: (i, 0)),
        _resident(whg.shape), _resident(wcv.shape), _resident(wout.shape),
        vec(D), pl.BlockSpec((CONV_K, Cw), lambda i: (0, 0)), vec(Cw), vec(Cw), vec(Cw),
        vec(D), vec(D), vec(D), vec(D),
    ]
    return pl.pallas_call(
        functools.partial(_mix_kernel, tiles_per_seq=seq // tm, alpha=alpha),
        grid=(nt,),
        in_specs=in_specs,
        out_specs=pl.BlockSpec((tm, D), lambda i: (i, 0)),
        out_shape=jax.ShapeDtypeStruct((T, D), F32),
        scratch_shapes=[pltpu.VMEM((tm + 2 * halo, Cw), F32), pltpu.VMEM((tm, Cw), F32)],
        compiler_params=_cparams(("parallel",)),
        name="mix",
    )(o_f, o_b, P, P, P, P, P, P, x, whg, wcv, wout, *small)


def _route_kernel(x1_ref, p_ref, wpg_ref, wpp_ref, wr_ref, br_ref, base_ref, xp_ref, rt_ref, cnt_ref,
                  *, alpha):
    tm, D = x1_ref.shape
    E = wr_ref.shape[1]

    @pl.when(pl.program_id(0) == 0)
    def _():
        cnt_ref[...] = jnp.zeros_like(cnt_ref)

    x1 = x1_ref[...]
    xb = x1.astype(BF16)
    gate = _sigmoid(jnp.dot(xb, wpg_ref[...], preferred_element_type=F32))
    ple = gate * jnp.dot(p_ref[...].astype(BF16), wpp_ref[...], preferred_element_type=F32)
    base_ref[...] = alpha * x1 + ple
    xp_ref[...] = _pack_bf16_pair(x1[:, :D // 2], x1[:, D // 2:])

    logits = jnp.dot(x1, wr_ref[...], preferred_element_type=F32,
                     precision=lax.Precision.HIGHEST) + br_ref[...]
    col = lax.broadcasted_iota(I32, (tm, E), 1).astype(F32)
    lane = lax.broadcasted_iota(I32, (tm, LANES), 1)
    sels, vals = [], []
    work = logits
    for _ in range(TOP_K):
        mx = jnp.max(work, axis=-1, keepdims=True)
        idx = jnp.min(jnp.where(work == mx, col, float(E)), axis=-1, keepdims=True)
        sel = col == idx
        sels.append((sel, idx))
        vals.append(mx)
        work = jnp.where(sel, -jnp.inf, work)
    pf = sum(jnp.where(sel, 1.0, 0.0) for sel, _ in sels)
    r = lax.broadcasted_iota(I32, (tm, tm), 0)
    c = lax.broadcasted_iota(I32, (tm, tm), 1)
    before = jnp.dot(jnp.where(c < r, 1.0, 0.0).astype(BF16), pf.astype(BF16),
                     preferred_element_type=F32) + cnt_ref[...]
    es = [jnp.exp(v - vals[0]) for v in vals]
    den = es[0] + es[1] + es[2] + es[3]
    out = jnp.zeros((tm, LANES), F32)
    for kk in range(TOP_K):
        sel, idx = sels[kk]
        rank = jnp.sum(jnp.where(sel, before, 0.0), axis=-1, keepdims=True)
        out = jnp.where(lane == kk, idx, out)
        out = jnp.where(lane == TOP_K + kk, es[kk] / den, out)
        out = jnp.where(lane == 2 * TOP_K + kk, rank, out)
    rt_ref[...] = out
    cnt_ref[...] = cnt_ref[...] + jnp.sum(pf, axis=0, keepdims=True)


def _route(x1, p, wpg, wpp, wr, br, tm, alpha):
    T, D = x1.shape
    E = wr.shape[1]
    return pl.pallas_call(
        functools.partial(_route_kernel, alpha=alpha),
        grid=(T // tm,),
        in_specs=[
            pl.BlockSpec((tm, D), lambda i: (i, 0)),
            pl.BlockSpec((tm, p.shape[1]), lambda i: (i, 0)),
            _resident(wpg.shape), _resident(wpp.shape),
            pl.BlockSpec(wr.shape, lambda i: (0, 0)),
            pl.BlockSpec((1, E), lambda i: (0, 0)),
        ],
        out_specs=[
            pl.BlockSpec((tm, D), lambda i: (i, 0)),
            pl.BlockSpec((tm, D // 2), lambda i: (i, 0)),
            pl.BlockSpec((tm, LANES), lambda i: (i, 0)),
            pl.BlockSpec((1, E), lambda i: (0, 0)),
        ],
        out_shape=[
            jax.ShapeDtypeStruct((T, D), F32),
            jax.ShapeDtypeStruct((T, D // 2), U32),
            jax.ShapeDtypeStruct((T, LANES), F32),
            jax.ShapeDtypeStruct((1, E), F32),
        ],
        compiler_params=_cparams(("arbitrary",)),
        name="route",
    )(x1, p, wpg, wpp, wr, br)


def _row_copy(src, s, dst, d, sem):
    return pltpu.make_async_copy(src.at[pl.ds(s, 1), :], dst.at[pl.ds(d, 1), :], sem)


def _scatter_kernel(pos_ref, src_ref, init_ref, dst_ref, sem, *, per_step):
    del init_ref
    base = pl.program_id(0) * (per_step // TOP_K)

    def issue(t, carry):
        for kk in range(TOP_K):
            _row_copy(src_ref, base + t, dst_ref, pos_ref[t * TOP_K + kk], sem).start()
        return carry

    lax.fori_loop(0, per_step // TOP_K, issue, 0)

    def drain(n, carry):
        _row_copy(src_ref, 0, dst_ref, 0, sem).wait()
        return carry

    lax.fori_loop(0, per_step, drain, 0)


def _scatter_rows(pos_flat, src, n_dst, per_step):
    n = pos_flat.shape[0]
    init = jnp.zeros((n_dst, src.shape[1]), src.dtype)
    return pl.pallas_call(
        functools.partial(_scatter_kernel, per_step=per_step),
        grid=(n // per_step,),
        in_specs=[
            pl.BlockSpec((per_step,), lambda i: (i,), memory_space=pltpu.SMEM),
            pl.BlockSpec(memory_space=pl.ANY),
            pl.BlockSpec(memory_space=pl.ANY),
        ],
        out_specs=pl.BlockSpec(memory_space=pl.ANY),
        out_shape=jax.ShapeDtypeStruct(init.shape, init.dtype),
        scratch_shapes=[pltpu.SemaphoreType.DMA(())],
        input_output_aliases={2: 0},
        compiler_params=pltpu.CompilerParams(dimension_semantics=("arbitrary",), has_side_effects=True),
        name="scatter_rows",
    )(pos_flat, src, init)


def _gather_kernel(pos_ref, src_ref, dst_ref, sem, *, per_step, n_tok):
    base = pl.program_id(0) * (per_step // TOP_K)

    def issue(t, carry):
        for kk in range(TOP_K):
            _row_copy(src_ref, pos_ref[t * TOP_K + kk], dst_ref, kk * n_tok + base + t, sem).start()
        return carry

    lax.fori_loop(0, per_step // TOP_K, issue, 0)

    def drain(n, carry):
        _row_copy(src_ref, 0, dst_ref, 0, sem).wait()
        return carry

    lax.fori_loop(0, per_step, drain, 0)


def _gather_rows(pos_flat, src, per_step):
    n = pos_flat.shape[0]
    return pl.pallas_call(
        functools.partial(_gather_kernel, per_step=per_step, n_tok=n // TOP_K),
        grid=(n // per_step,),
        in_specs=[
            pl.BlockSpec((per_step,), lambda i: (i,), memory_space=pltpu.SMEM),
            pl.BlockSpec(memory_space=pl.ANY),
        ],
        out_specs=pl.BlockSpec(memory_space=pl.ANY),
        out_shape=jax.ShapeDtypeStruct((n, src.shape[1]), src.dtype),
        scratch_shapes=[pltpu.SemaphoreType.DMA(())],
        compiler_params=pltpu.CompilerParams(dimension_semantics=("arbitrary",), has_side_effects=True),
        name="gather_rows",
    )(pos_flat, src)


def _weights_changed(te_ref):
    m = pl.program_id(1)
    return (m == 0) | (te_ref[m] != te_ref[jnp.maximum(m - 1, 0)])


def _gm1_kernel(te_ref, tv_ref, xs_ref, wg_ref, wl_ref, bg_ref, bl_ref, h_ref, wgb_ref, wlb_ref):
    m = pl.program_id(1)

    @pl.when(_weights_changed(te_ref))
    def _():
        wgb_ref[...] = wg_ref[...].astype(BF16)
        wlb_ref[...] = wl_ref[...].astype(BF16)

    @pl.when(tv_ref[m] != 0)
    def _():
        hi, lo = _unpack_bf16_pair(xs_ref[...])
        x = jnp.concatenate([hi, lo], axis=-1).astype(BF16)
        hg = jnp.dot(x, wgb_ref[...], preferred_element_type=F32) + bg_ref[...]
        hl = jnp.dot(x, wlb_ref[...], preferred_element_type=F32) + bl_ref[...]
        hg = jnp.minimum(hg, SWIGLU_LIMIT)
        hl = jnp.clip(hl, -SWIGLU_LIMIT, SWIGLU_LIMIT)
        h_ref[...] = (hg * _sigmoid(SWIGLU_ALPHA * hg) * (hl + 1.0)).astype(BF16)

    @pl.when(tv_ref[m] == 0)
    def _():
        h_ref[...] = jnp.zeros_like(h_ref)


def _gm1(tile_e, tile_v, xs, w1, b1, tme, tn):
    Np = xs.shape[0]
    E, D, F2 = w1.shape
    Fh = F2 // 2
    nj = Fh // tn
    grid_spec = pltpu.PrefetchScalarGridSpec(
        num_scalar_prefetch=2,
        grid=(nj, Np // tme),
        in_specs=[
            pl.BlockSpec((tme, xs.shape[1]), lambda j, m, te, tv: (m, 0)),
            pl.BlockSpec((None, D, tn), lambda j, m, te, tv: (te[m], 0, j)),
            pl.BlockSpec((None, D, tn), lambda j, m, te, tv: (te[m], 0, nj + j)),
            pl.BlockSpec((None, 1, tn), lambda j, m, te, tv: (te[m], 0, j)),
            pl.BlockSpec((None, 1, tn), lambda j, m, te, tv: (te[m], 0, nj + j)),
        ],
        out_specs=pl.BlockSpec((tme, tn), lambda j, m, te, tv: (m, j)),
        scratch_shapes=[pltpu.VMEM((D, tn), BF16), pltpu.VMEM((D, tn), BF16)],
    )
    return pl.pallas_call(
        _gm1_kernel,
        grid_spec=grid_spec,
        out_shape=jax.ShapeDtypeStruct((Np, Fh), BF16),
        compiler_params=_cparams(("arbitrary", "arbitrary")),
        name="gm1",
    )(tile_e, tile_v, xs, w1, w1, b1, b1)


def _gm2_kernel(te_ref, tv_ref, h_ref, w_ref, b_ref, y_ref, wb_ref):
    m = pl.program_id(1)
    half = y_ref.shape[1]

    @pl.when(_weights_changed(te_ref))
    def _():
        wb_ref[...] = w_ref[...].astype(BF16)

    @pl.when(tv_ref[m] != 0)
    def _():
        y = jnp.dot(h_ref[...], wb_ref[...], preferred_element_type=F32) + b_ref[...]
        y_ref[...] = _pack_bf16_pair(y[:, :half], y[:, half:])

    @pl.when(tv_ref[m] == 0)
    def _():
        y_ref[...] = jnp.zeros_like(y_ref)


def _gm2(tile_e, tile_v, hs, w2, b2, tme, tn):
    Np, Fh = hs.shape
    E, _, D = w2.shape
    grid_spec = pltpu.PrefetchScalarGridSpec(
        num_scalar_prefetch=2,
        grid=(D // tn, Np // tme),
        in_specs=[
            pl.BlockSpec((tme, Fh), lambda j, m, te, tv: (m, 0)),
            pl.BlockSpec((None, Fh, tn), lambda j, m, te, tv: (te[m], 0, j)),
            pl.BlockSpec((None, 1, tn), lambda j, m, te, tv: (te[m], 0, j)),
        ],
        out_specs=pl.BlockSpec((tme, tn // 2), lambda j, m, te, tv: (m, j)),
        scratch_shapes=[pltpu.VMEM((Fh, tn), BF16)],
    )
    return pl.pallas_call(
        _gm2_kernel,
        grid_spec=grid_spec,
        out_shape=jax.ShapeDtypeStruct((Np, D // 2), U32),
        compiler_params=_cparams(("arbitrary", "arbitrary")),
        name="gm2",
    )(tile_e, tile_v, hs, w2, b2)


def _final_kernel(base_ref, y_ref, rt_ref, g_ref, b_ref, o_ref, *, tn):
    acc = base_ref[...]
    rt = rt_ref[...]
    half = tn // 2
    for kk in range(TOP_K):
        w = rt[:, TOP_K + kk:TOP_K + kk + 1]
        parts = []
        for j in range(y_ref.shape[2] // half):
            hi, lo = _unpack_bf16_pair(y_ref[kk, :, j * half:(j + 1) * half])
            parts += [hi, lo]
        acc = acc + w * jnp.concatenate(parts, axis=-1)
    o_ref[...] = _ln(acc, g_ref[...], b_ref[...])


def _final(base, y4, rt, g, b, tm, tn):
    T, D = base.shape
    return pl.pallas_call(
        functools.partial(_final_kernel, tn=tn),
        grid=(T // tm,),
        in_specs=[
            pl.BlockSpec((tm, D), lambda i: (i, 0)),
            pl.BlockSpec((TOP_K, tm, D // 2), lambda i: (0, i, 0)),
            pl.BlockSpec((tm, LANES), lambda i: (i, 0)),
            pl.BlockSpec((1, D), lambda i: (0, 0)),
            pl.BlockSpec((1, D), lambda i: (0, 0)),
        ],
        out_specs=pl.BlockSpec((tm, D), lambda i: (i, 0)),
        out_shape=jax.ShapeDtypeStruct((T, D), F32),
        compiler_params=_cparams(("parallel",)),
        name="final",
    )(base, y4, rt, g, b)


def _tile_sizes(T, seq, D):
    small = seq < 512
    return dict(
        tm_in=min(512, seq), tn_in=min(1024, 8 * D),
        tm_mix=min(256, seq), tm_route=min(256, seq), tm_final=min(512, seq),
        tme=128 if small else 512, tn_e=min(512, D),
        per_step=1024 if small else 2048,
    )


def _encoder_layer(x, p, n_seq, seq, layer, depth, ln_emb_g, ln_emb_b, w_in, lower_bounds, hg_norm_g,
                   w_hg_out, conv_w, conv_b, conv_ln_g, conv_ln_b, w_conv_out, w_out, ln1_g, ln1_b,
                   w_router, b_router, w_exp1, b_exp1, w_exp2, b_exp2, w_ple_gate, w_ple_proj,
                   ln2_g, ln2_b):
    T, D = x.shape
    E = w_router.shape[1]
    H = D // LANES
    ts = _tile_sizes(T, seq, D)
    alpha = (2.0 * depth) ** 0.25
    row = lambda v: v.reshape(1, -1)

    P = _inproj(x, row(ln_emb_g), row(ln_emb_b), w_in.astype(BF16), ts["tm_in"], ts["tn_in"])
    lbw = lower_bounds.reshape(2, lower_bounds.shape[1], H, 1, LANES)
    o_f, o_b = _scan(P, lbw, n_seq, seq, layer)
    small = (row(hg_norm_g), conv_w, row(conv_b), row(conv_ln_g), row(conv_ln_b),
             row(ln_emb_g), row(ln_emb_b), row(ln1_g), row(ln1_b))
    x1 = _mix(o_f, o_b, P, x, w_hg_out.astype(BF16), w_conv_out.astype(BF16), w_out.astype(BF16),
              small, seq, ts["tm_mix"], alpha)
    base, xp, rt, counts = _route(x1, p, w_ple_gate.astype(BF16), w_ple_proj.astype(BF16),
                                  w_router, row(b_router), ts["tm_route"], alpha)

    tme = ts["tme"]
    n_tiles = (T * TOP_K) // tme + E
    cnt = counts[0].astype(I32)
    padded = ((cnt + tme - 1) // tme) * tme
    ends = jnp.cumsum(padded)
    starts = ends - padded
    idx = rt[:, :TOP_K].astype(I32)
    rank = rt[:, 2 * TOP_K:3 * TOP_K].astype(I32)
    pos = (starts[idx] + rank).reshape(-1)
    tile_start = jnp.arange(n_tiles, dtype=I32) * tme
    tile_e = jnp.minimum(jnp.searchsorted(ends, tile_start, side="right"), E - 1).astype(I32)
    tile_v = (tile_start < ends[-1]).astype(I32)

    xs = _scatter_rows(pos, xp, n_tiles * tme, ts["per_step"])
    hs = _gm1(tile_e, tile_v, xs, w_exp1, b_exp1.reshape(E, 1, -1), tme, ts["tn_e"])
    ys = _gm2(tile_e, tile_v, hs, w_exp2, b_exp2.reshape(E, 1, -1), tme, ts["tn_e"])
    y4 = _gather_rows(pos, ys, ts["per_step"]).reshape(TOP_K, T, D // 2)
    return _final(base, y4, rt, row(ln2_g), row(ln2_b), ts["tm_final"], ts["tn_e"])


def kernel(x_prompt, x_sample, p_prompt, p_sample, ln_emb_g, ln_emb_b, w_in, lower_bounds, hg_norm_g, w_hg_out, conv_w, conv_b, conv_ln_g, conv_ln_b, w_conv_out, w_out, ln1_g, ln1_b, w_router, b_router, w_exp1, b_exp1, w_exp2, b_exp2, w_ple_gate, w_ple_proj, ln2_g, ln2_b):
    depth = w_in.shape[0]
    assert depth == 1, "the per-layer input embeddings and weights are wired for a single layer"
    D = x_prompt.shape[-1]
    seq = x_prompt.shape[1]
    assert x_sample.shape[1] == seq
    nb_p, nb_s = x_prompt.shape[0], x_sample.shape[0]
    x = jnp.concatenate([x_prompt.reshape(-1, D), x_sample.reshape(-1, D)], axis=0)
    p = jnp.concatenate([p_prompt[0].reshape(-1, p_prompt.shape[-1]),
                         p_sample[0].reshape(-1, p_sample.shape[-1])], axis=0)
    y = _encoder_layer(x, p, nb_p + nb_s, seq, 0, depth, ln_emb_g, ln_emb_b, w_in[0], lower_bounds,
                       hg_norm_g[0], w_hg_out[0], conv_w[0], conv_b[0], conv_ln_g[0], conv_ln_b[0],
                       w_conv_out[0], w_out[0], ln1_g[0], ln1_b[0], w_router[0], b_router[0],
                       w_exp1[0], b_exp1[0], w_exp2[0], b_exp2[0], w_ple_gate[0], w_ple_proj[0],
                       ln2_g[0], ln2_b[0])
    n_p = nb_p * seq
    return (y[:n_p].reshape(x_prompt.shape), y[n_p:].reshape(x_sample.shape))
```

```python
import functools
import math

import jax
import jax.numpy as jnp
from jax import lax
from jax.experimental import pallas as pl
from jax.experimental.pallas import tpu as pltpu

F32 = jnp.float32
BF16 = jnp.bfloat16
U32 = jnp.uint32
I32 = jnp.int32

LANES = 128
NORM_EPS = 1e-5
CONV_K = 31
TOP_K = 4
SWIGLU_ALPHA = 1.702
SWIGLU_LIMIT = 7.0
SCAN_CHUNK = 64
DIAG = 8
MAX_LOG_DECAY = 80.0
VMEM_LIMIT = 56 * 1024 * 1024


def _cparams(sem, vmem=VMEM_LIMIT):
    return pltpu.CompilerParams(dimension_semantics=sem, vmem_limit_bytes=vmem)


def _resident(shape):
    nd = len(shape)
    return pl.BlockSpec(shape, lambda *_: (0,) * nd, pipeline_mode=pl.Buffered(1))


def _ln(x, g, b):
    mu = jnp.mean(x, axis=-1, keepdims=True)
    xc = x - mu
    var = jnp.mean(xc * xc, axis=-1, keepdims=True)
    return xc * lax.rsqrt(var + NORM_EPS) * g + b


def _sigmoid(x):
    return 1.0 / (1.0 + jnp.exp(-x))


def _pack_bf16_pair(hi, lo):
    hb = pltpu.bitcast(hi.astype(BF16).astype(F32), U32)
    lb = pltpu.bitcast(lo.astype(BF16).astype(F32), U32)
    return (hb & jnp.uint32(0xFFFF0000)) | (lb >> 16)


def _unpack_bf16_pair(w):
    hi = pltpu.bitcast(w & jnp.uint32(0xFFFF0000), F32)
    lo = pltpu.bitcast(w << 16, F32)
    return hi, lo


def _inproj_kernel(x_ref, g_ref, b_ref, w_ref, o_ref, h_ref):
    @pl.when(pl.program_id(1) == 0)
    def _():
        h_ref[...] = _ln(x_ref[...], g_ref[...], b_ref[...]).astype(BF16)

    acc = jnp.dot(h_ref[...], w_ref[...], preferred_element_type=F32)
    for c in range(o_ref.shape[0]):
        o_ref[c] = acc[:, c * LANES:(c + 1) * LANES].astype(BF16)


def _inproj(x, g, b, w_bf, tm, tn):
    T, D = x.shape
    N = w_bf.shape[1]
    return pl.pallas_call(
        _inproj_kernel,
        grid=(T // tm, N // tn),
        in_specs=[
            pl.BlockSpec((tm, D), lambda i, j: (i, 0)),
            pl.BlockSpec((1, D), lambda i, j: (0, 0)),
            pl.BlockSpec((1, D), lambda i, j: (0, 0)),
            pl.BlockSpec((D, tn), lambda i, j: (0, j)),
        ],
        out_specs=pl.BlockSpec((tn // LANES, tm, LANES), lambda i, j: (j, i, 0)),
        out_shape=jax.ShapeDtypeStruct((N // LANES, T, LANES), BF16),
        scratch_shapes=[pltpu.VMEM((tm, D), BF16)],
        compiler_params=_cparams(("parallel", "arbitrary")),
        name="inproj",
    )(x, g, b, w_bf)


def _cumsum_rows(g, tri):
    g1 = g.astype(BF16)
    r1 = g - g1.astype(F32)
    g2 = r1.astype(BF16)
    g3 = (r1 - g2.astype(F32)).astype(BF16)
    out = jnp.dot(tri, g1, preferred_element_type=F32)
    out = out + jnp.dot(tri, g2, preferred_element_type=F32)
    return out + jnp.dot(tri, g3, preferred_element_type=F32)


def _scan_chunk(q, v_bf, vt_bf, fl, lb, st, rev, bounded):
    C = q.shape[0]
    gate = lb + (1.0 - lb) * _sigmoid(fl)
    k = 1.0 - gate
    g = jnp.log(gate)
    row = lax.broadcasted_iota(I32, (C, C), 0)
    col = lax.broadcasted_iota(I32, (C, C), 1)
    tri = (col >= row) if rev else (col <= row)
    b = _cumsum_rows(g, jnp.where(tri, 1.0, 0.0).astype(BF16))
    b_end = b[0:1, :] if rev else b[C - 1:C, :]
    qe = (q * jnp.exp(b)).astype(BF16)

    if bounded:
        kt = k * jnp.exp(-b)
        s_all = lax.dot_general(qe, kt.astype(BF16), (((1,), (1,)), ((), ())), preferred_element_type=F32)
        scores = jnp.where(tri, s_all, 0.0)
        kdec = (kt * jnp.exp(b_end)).astype(BF16)
    else:
        scores = _scores_unbounded(q, k, gate, b, row, col, rev)
        kdec = (k * jnp.exp(b_end - b)).astype(BF16)

    o = jnp.dot(scores.astype(BF16), v_bf, preferred_element_type=F32)
    o = o + lax.dot_general(qe, st.astype(BF16), (((1,), (1,)), ((), ())), preferred_element_type=F32)
    st_new = st * jnp.exp(b_end) + jnp.dot(vt_bf, kdec, preferred_element_type=F32)
    return o, st_new


def _scores_unbounded(q, k, gate, b, row, col, rev):
    C = q.shape[0]
    scores = jnp.zeros((C, C), F32)
    m = C // 2
    while m >= DIAG:
        n = C // m
        bm = b.reshape(n, m, b.shape[-1])
        zero = jnp.zeros((1, 1, b.shape[-1]), F32)
        if rev:
            edge = bm[:, 0:1, :]
            ref_q = jnp.concatenate([edge[1:], zero], axis=0)
        else:
            edge = bm[:, m - 1:m, :]
            ref_q = jnp.concatenate([zero, edge[:-1]], axis=0)
        eq = (bm - ref_q).reshape(C, -1)
        ek = (edge - bm).reshape(C, -1)
        ql = (q * jnp.exp(eq)).astype(BF16)
        kl = (k * jnp.exp(ek)).astype(BF16)
        s_l = lax.dot_general(ql, kl, (((1,), (1,)), ((), ())), preferred_element_type=F32)
        rb, cb = row // m, col // m
        if rev:
            mask = (cb == rb + 1) & ((rb & 1) == 0)
        else:
            mask = (rb == cb + 1) & ((cb & 1) == 0)
        scores = scores + jnp.where(mask, s_l, 0.0)
        m //= 2

    rin = row & (DIAG - 1)
    gprod = None
    for d in range(DIAG):
        if d == 0:
            a = q * k
            mask = row == col
        else:
            sh = (C - d) if rev else d
            gprod = gate if d == 1 else gate * pltpu.roll(gprod, (C - 1) if rev else 1, axis=0)
            a = q * pltpu.roll(k, sh, axis=0) * gprod
            if rev:
                mask = (col == row + d) & (rin + d < DIAG)
            else:
                mask = (col == row - d) & (rin >= d)
        scores = scores + jnp.where(mask, jnp.sum(a, axis=-1, keepdims=True), 0.0)
    return scores


def _scan_kernel(qf_ref, vf_ref, ff_ref, qb_ref, vb_ref, fb_ref, lbw_ref, of_ref, ob_ref, sf_ref, sb_ref,
                 *, layer):
    @pl.when(pl.program_id(1) == 0)
    def _():
        sf_ref[...] = jnp.zeros_like(sf_ref)
        sb_ref[...] = jnp.zeros_like(sb_ref)

    def lower_bounds(raw):
        e = jnp.exp(raw - jnp.max(raw, axis=-3, keepdims=True))
        return jnp.sum(e[..., :layer + 1, :, :], axis=-3) / jnp.sum(e, axis=-3)

    def run(bounded):
        def body(h, carry):
            for rev, (q_ref, v_ref, f_ref, o_ref, s_ref) in enumerate(
                    ((qf_ref, vf_ref, ff_ref, of_ref, sf_ref), (qb_ref, vb_ref, fb_ref, ob_ref, sb_ref))):
                qr = q_ref[h].astype(F32)
                q = qr * _sigmoid(qr)
                v = v_ref[h].astype(F32)
                o, st = _scan_chunk(q, v.astype(BF16), v.T.astype(BF16), f_ref[h].astype(F32),
                                    lower_bounds(lbw_ref[rev, h]), s_ref[h], bool(rev), bounded)
                o_ref[h] = o.astype(BF16)
                s_ref[h] = st
            return carry

        lax.fori_loop(0, qf_ref.shape[0], body, 0, unroll=2 if bounded else 1)

    lb_all = lower_bounds(lbw_ref[...])
    lb_min = jnp.min(jnp.min(lb_all, axis=(0, 1)))
    is_bounded = lb_min > math.exp(-MAX_LOG_DECAY / SCAN_CHUNK)
    pl.when(is_bounded)(lambda: run(True))
    pl.when(jnp.logical_not(is_bounded))(lambda: run(False))


def _scan(P, lbw, n_seq, seq, layer):
    H = lbw.shape[1]
    T = P.shape[1]
    C = SCAN_CHUNK
    nc = seq // C

    def spec(sec, rev):
        if rev:
            return pl.BlockSpec((H, C, LANES), lambda s, c: (sec, s * nc + nc - 1 - c, 0))
        return pl.BlockSpec((H, C, LANES), lambda s, c: (sec, s * nc + c, 0))

    out_sds = jax.ShapeDtypeStruct((H, T, LANES), BF16)
    return pl.pallas_call(
        functools.partial(_scan_kernel, layer=layer),
        grid=(n_seq, nc),
        in_specs=[spec(0, False), spec(3, False), spec(1, False),
                  spec(0, True), spec(3, True), spec(2, True),
                  pl.BlockSpec(lbw.shape, lambda s, c: (0,) * lbw.ndim)],
        out_specs=[spec(0, False), spec(0, True)],
        out_shape=[out_sds, out_sds],
        scratch_shapes=[pltpu.VMEM((H, LANES, LANES), F32), pltpu.VMEM((H, LANES, LANES), F32)],
        compiler_params=_cparams(("parallel", "arbitrary")),
        name="scan",
    )(P, P, P, P, P, P, lbw)


def _mix_kernel(of_ref, ob_ref, og_ref, glu_ref, gprev_ref, gnext_ref, ga_ref, gc_ref, x_ref,
                whg_ref, wcv_ref, wout_ref, hgn_ref, cw_ref, cb_ref, clg_ref, clb_ref,
                eg_ref, eb_ref, l1g_ref, l1b_ref, x1_ref, hc_ref, cv_ref, *, tiles_per_seq, alpha):
    H, tm, _ = of_ref.shape
    nch = glu_ref.shape[0] // 2
    halo = gprev_ref.shape[1]
    i = pl.program_id(0)
    first = (i % tiles_per_seq) == 0
    last = (i % tiles_per_seq) == tiles_per_seq - 1

    def cat(ref, lo=0, hi=None):
        hi = ref.shape[0] if hi is None else hi
        return jnp.concatenate([ref[c].astype(F32) for c in range(lo, hi)], axis=-1)

    o = cat(of_ref) + cat(ob_ref)
    og = cat(og_ref)
    on = o * lax.rsqrt(jnp.mean(o * o, axis=-1, keepdims=True) + NORM_EPS) * hgn_ref[...]
    a = jnp.dot((on * (og * _sigmoid(og))).astype(BF16), whg_ref[...], preferred_element_type=F32)

    def glu(ref):
        return cat(ref, 0, nch) * _sigmoid(cat(ref, nch, 2 * nch))

    hc_ref[pl.ds(0, halo), :] = jnp.where(first, 0.0, glu(gprev_ref))
    hc_ref[pl.ds(halo, tm), :] = glu(glu_ref)
    hc_ref[pl.ds(halo + tm, halo), :] = jnp.where(last, 0.0, glu(gnext_ref))
    rb = 32
    for c in range(nch):
        cs = slice(c * LANES, (c + 1) * LANES)
        taps = [cw_ref[j:j + 1, cs] for j in range(CONV_K)]
        for r0 in range(0, tm, rb):
            acc = jnp.zeros((rb, LANES), F32)
            for j in range(CONV_K):
                lo = r0 + halo - CONV_K // 2 + j
                acc = acc + taps[j] * hc_ref[lo:lo + rb, cs]
            cv_ref[r0:r0 + rb, cs] = acc
    cn = _ln(cv_ref[...] + cb_ref[...], clg_ref[...], clb_ref[...])
    cc = jnp.dot((cn * _sigmoid(cn)).astype(BF16), wcv_ref[...], preferred_element_type=F32)

    mixed = jnp.dot((_sigmoid(cat(ga_ref)) * a + _sigmoid(cat(gc_ref)) * cc).astype(BF16), wout_ref[...],
                    preferred_element_type=F32)
    h = _ln(x_ref[...], eg_ref[...], eb_ref[...])
    x1_ref[...] = _ln(alpha * h + mixed, l1g_ref[...], l1b_ref[...])


def _mix(o_f, o_b, P, x, whg, wcv, wout, small, seq, tm, alpha):
    H, T, _ = o_f.shape
    D = x.shape[1]
    Cw = wcv.shape[0]
    halo = 16
    nt = T // tm
    hb = tm // halo
    last_hb = T // halo - 1
    sec = lambda s: pl.BlockSpec((H, tm, LANES), lambda i: (s, i, 0))
    vec = lambda n: pl.BlockSpec((1, n), lambda i: (0, 0))
    in_specs = [
        sec(0), sec(0), sec(4), sec(5),
        pl.BlockSpec((H, halo, LANES), lambda i: (5, jnp.maximum(i * hb - 1, 0), 0)),
        pl.BlockSpec((H, halo, LANES), lambda i: (5, jnp.minimum((i + 1) * hb, last_hb), 0)),
        sec(6), sec(7),
        pl.BlockSpec((tm, D), lambda i: (i, 0)),
        _resident(whg.shape), _resident(wcv.shape), _resident(wout.shape),
        vec(D), pl.BlockSpec((CONV_K, Cw), lambda i: (0, 0)), vec(Cw), vec(Cw), vec(Cw),
        vec(D), vec(D), vec(D), vec(D),
    ]
    return pl.pallas_call(
        functools.partial(_mix_kernel, tiles_per_seq=seq // tm, alpha=alpha),
        grid=(nt,),
        in_specs=in_specs,
        out_specs=pl.BlockSpec((tm, D), lambda i: (i, 0)),
        out_shape=jax.ShapeDtypeStruct((T, D), F32),
        scratch_shapes=[pltpu.VMEM((tm + 2 * halo, Cw), F32), pltpu.VMEM((tm, Cw), F32)],
        compiler_params=_cparams(("parallel",)),
        name="mix",
    )(o_f, o_b, P, P, P, P, P, P, x, whg, wcv, wout, *small)


def _route_kernel(x1_ref, p_ref, wpg_ref, wpp_ref, wr_ref, br_ref, base_ref, xp_ref, rt_ref, cnt_ref,
                  *, alpha):
    tm, D = x1_ref.shape
    E = wr_ref.shape[1]

    @pl.when(pl.program_id(0) == 0)
    def _():
        cnt_ref[...] = jnp.zeros_like(cnt_ref)

    x1 = x1_ref[...]
    xb = x1.astype(BF16)
    gate = _sigmoid(jnp.dot(xb, wpg_ref[...], preferred_element_type=F32))
    ple = gate * jnp.dot(p_ref[...].astype(BF16), wpp_ref[...], preferred_element_type=F32)
    base_ref[...] = alpha * x1 + ple
    xp_ref[...] = _pack_bf16_pair(x1[:, :D // 2], x1[:, D // 2:])

    logits = jnp.dot(x1, wr_ref[...], preferred_element_type=F32,
                     precision=lax.Precision.HIGHEST) + br_ref[...]
    col = lax.broadcasted_iota(I32, (tm, E), 1).astype(F32)
    lane = lax.broadcasted_iota(I32, (tm, LANES), 1)
    sels, vals = [], []
    work = logits
    for _ in range(TOP_K):
        mx = jnp.max(work, axis=-1, keepdims=True)
        idx = jnp.min(jnp.where(work == mx, col, float(E)), axis=-1, keepdims=True)
        sel = col == idx
        sels.append((sel, idx))
        vals.append(mx)
        work = jnp.where(sel, -jnp.inf, work)
    pf = sum(jnp.where(sel, 1.0, 0.0) for sel, _ in sels)
    r = lax.broadcasted_iota(I32, (tm, tm), 0)
    c = lax.broadcasted_iota(I32, (tm, tm), 1)
    before = jnp.dot(jnp.where(c < r, 1.0, 0.0).astype(BF16), pf.astype(BF16),
                     preferred_element_type=F32) + cnt_ref[...]
    es = [jnp.exp(v - vals[0]) for v in vals]
    den = es[0] + es[1] + es[2] + es[3]
    out = jnp.zeros((tm, LANES), F32)
    for kk in range(TOP_K):
        sel, idx = sels[kk]
        rank = jnp.sum(jnp.where(sel, before, 0.0), axis=-1, keepdims=True)
        out = jnp.where(lane == kk, idx, out)
        out = jnp.where(lane == TOP_K + kk, es[kk] / den, out)
        out = jnp.where(lane == 2 * TOP_K + kk, rank, out)
    rt_ref[...] = out
    cnt_ref[...] = cnt_ref[...] + jnp.sum(pf, axis=0, keepdims=True)


def _route(x1, p, wpg, wpp, wr, br, tm, alpha):
    T, D = x1.shape
    E = wr.shape[1]
    return pl.pallas_call(
        functools.partial(_route_kernel, alpha=alpha),
        grid=(T // tm,),
        in_specs=[
            pl.BlockSpec((tm, D), lambda i: (i, 0)),
            pl.BlockSpec((tm, p.shape[1]), lambda i: (i, 0)),
            _resident(wpg.shape), _resident(wpp.shape),
            pl.BlockSpec(wr.shape, lambda i: (0, 0)),
            pl.BlockSpec((1, E), lambda i: (0, 0)),
        ],
        out_specs=[
            pl.BlockSpec((tm, D), lambda i: (i, 0)),
            pl.BlockSpec((tm, D // 2), lambda i: (i, 0)),
            pl.BlockSpec((tm, LANES), lambda i: (i, 0)),
            pl.BlockSpec((1, E), lambda i: (0, 0)),
        ],
        out_shape=[
            jax.ShapeDtypeStruct((T, D), F32),
            jax.ShapeDtypeStruct((T, D // 2), U32),
            jax.ShapeDtypeStruct((T, LANES), F32),
            jax.ShapeDtypeStruct((1, E), F32),
        ],
        compiler_params=_cparams(("arbitrary",)),
        name="route",
    )(x1, p, wpg, wpp, wr, br)


def _row_copy(src, s, dst, d, sem):
    return pltpu.make_async_copy(src.at[pl.ds(s, 1), :], dst.at[pl.ds(d, 1), :], sem)


def _drain_rows(src, dst, sem, n):
    def drain(_, carry):
        _row_copy(src, 0, dst, 0, sem).wait()
        return carry

    lax.fori_loop(0, n, drain, 0)


def _scatter_kernel(pos_ref, src_ref, init_ref, dst_ref, sem):
    del init_ref
    tm = src_ref.shape[0]

    def issue(t, carry):
        for kk in range(TOP_K):
            _row_copy(src_ref, t, dst_ref, pos_ref[t * TOP_K + kk], sem).start()
        return carry

    lax.fori_loop(0, tm, issue, 0)
    _drain_rows(dst_ref, dst_ref, sem, tm * TOP_K)


def _scatter_rows(pos_flat, src, n_dst, tm):
    T, W = src.shape
    init = jnp.zeros((n_dst, W), src.dtype)
    return pl.pallas_call(
        _scatter_kernel,
        grid=(T // tm,),
        in_specs=[
            pl.BlockSpec((tm * TOP_K,), lambda i: (i,), memory_space=pltpu.SMEM),
            pl.BlockSpec((tm, W), lambda i: (i, 0)),
            pl.BlockSpec(memory_space=pl.ANY),
        ],
        out_specs=pl.BlockSpec(memory_space=pl.ANY),
        out_shape=jax.ShapeDtypeStruct(init.shape, init.dtype),
        scratch_shapes=[pltpu.SemaphoreType.DMA(())],
        input_output_aliases={2: 0},
        compiler_params=pltpu.CompilerParams(dimension_semantics=("arbitrary",), has_side_effects=True),
        name="scatter_rows",
    )(pos_flat, src, init)


def _weights_changed(te_ref):
    m = pl.program_id(1)
    return (m == 0) | (te_ref[m] != te_ref[jnp.maximum(m - 1, 0)])


def _gm1_kernel(te_ref, tv_ref, xs_ref, wg_ref, wl_ref, bg_ref, bl_ref, h_ref, wgb_ref, wlb_ref):
    m = pl.program_id(1)

    @pl.when(_weights_changed(te_ref))
    def _():
        wgb_ref[...] = wg_ref[...].astype(BF16)
        wlb_ref[...] = wl_ref[...].astype(BF16)

    @pl.when(tv_ref[m] != 0)
    def _():
        hi, lo = _unpack_bf16_pair(xs_ref[...])
        x = jnp.concatenate([hi, lo], axis=-1).astype(BF16)
        hg = jnp.dot(x, wgb_ref[...], preferred_element_type=F32) + bg_ref[...]
        hl = jnp.dot(x, wlb_ref[...], preferred_element_type=F32) + bl_ref[...]
        hg = jnp.minimum(hg, SWIGLU_LIMIT)
        hl = jnp.clip(hl, -SWIGLU_LIMIT, SWIGLU_LIMIT)
        h_ref[...] = (hg * _sigmoid(SWIGLU_ALPHA * hg) * (hl + 1.0)).astype(BF16)

    @pl.when(tv_ref[m] == 0)
    def _():
        h_ref[...] = jnp.zeros_like(h_ref)


def _gm1(tile_e, tile_v, xs, w1, b1, tme, tn):
    Np = xs.shape[0]
    E, D, F2 = w1.shape
    Fh = F2 // 2
    nj = Fh // tn
    grid_spec = pltpu.PrefetchScalarGridSpec(
        num_scalar_prefetch=2,
        grid=(nj, Np // tme),
        in_specs=[
            pl.BlockSpec((tme, xs.shape[1]), lambda j, m, te, tv: (m, 0)),
            pl.BlockSpec((None, D, tn), lambda j, m, te, tv: (te[m], 0, j)),
            pl.BlockSpec((None, D, tn), lambda j, m, te, tv: (te[m], 0, nj + j)),
            pl.BlockSpec((None, 1, tn), lambda j, m, te, tv: (te[m], 0, j)),
            pl.BlockSpec((None, 1, tn), lambda j, m, te, tv: (te[m], 0, nj + j)),
        ],
        out_specs=pl.BlockSpec((tme, tn), lambda j, m, te, tv: (m, j)),
        scratch_shapes=[pltpu.VMEM((D, tn), BF16), pltpu.VMEM((D, tn), BF16)],
    )
    return pl.pallas_call(
        _gm1_kernel,
        grid_spec=grid_spec,
        out_shape=jax.ShapeDtypeStruct((Np, Fh), BF16),
        compiler_params=_cparams(("arbitrary", "arbitrary")),
        name="gm1",
    )(tile_e, tile_v, xs, w1, w1, b1, b1)


def _gm2_kernel(te_ref, tv_ref, h_ref, w_ref, b_ref, y_ref, wb_ref):
    m = pl.program_id(1)
    half = y_ref.shape[1]

    @pl.when(_weights_changed(te_ref))
    def _():
        wb_ref[...] = w_ref[...].astype(BF16)

    @pl.when(tv_ref[m] != 0)
    def _():
        y = jnp.dot(h_ref[...], wb_ref[...], preferred_element_type=F32) + b_ref[...]
        y_ref[...] = _pack_bf16_pair(y[:, :half], y[:, half:])

    @pl.when(tv_ref[m] == 0)
    def _():
        y_ref[...] = jnp.zeros_like(y_ref)


def _gm2(tile_e, tile_v, hs, w2, b2, tme, tn):
    Np, Fh = hs.shape
    E, _, D = w2.shape
    grid_spec = pltpu.PrefetchScalarGridSpec(
        num_scalar_prefetch=2,
        grid=(D // tn, Np // tme),
        in_specs=[
            pl.BlockSpec((tme, Fh), lambda j, m, te, tv: (m, 0)),
            pl.BlockSpec((None, Fh, tn), lambda j, m, te, tv: (te[m], 0, j)),
            pl.BlockSpec((None, 1, tn), lambda j, m, te, tv: (te[m], 0, j)),
        ],
        out_specs=pl.BlockSpec((tme, tn // 2), lambda j, m, te, tv: (m, j)),
        scratch_shapes=[pltpu.VMEM((Fh, tn), BF16)],
    )
    return pl.pallas_call(
        _gm2_kernel,
        grid_spec=grid_spec,
        out_shape=jax.ShapeDtypeStruct((Np, D // 2), U32),
        compiler_params=_cparams(("arbitrary", "arbitrary")),
        name="gm2",
    )(tile_e, tile_v, hs, w2, b2)


def _final_kernel(pos_ref, base_ref, ys_ref, rt_ref, g_ref, b_ref, o_ref, y_ref, sem, *, tn):
    tm = base_ref.shape[0]

    def issue(t, carry):
        for kk in range(TOP_K):
            _row_copy(ys_ref, pos_ref[t * TOP_K + kk], y_ref.at[kk], t, sem).start()
        return carry

    lax.fori_loop(0, tm, issue, 0)
    _drain_rows(ys_ref, y_ref.at[0], sem, tm * TOP_K)

    acc = base_ref[...]
    rt = rt_ref[...]
    half = tn // 2
    for kk in range(TOP_K):
        w = rt[:, TOP_K + kk:TOP_K + kk + 1]
        parts = []
        for j in range(y_ref.shape[2] // half):
            hi, lo = _unpack_bf16_pair(y_ref[kk, :, j * half:(j + 1) * half])
            parts += [hi, lo]
        acc = acc + w * jnp.concatenate(parts, axis=-1)
    o_ref[...] = _ln(acc, g_ref[...], b_ref[...])


def _final(pos_flat, base, ys, rt, g, b, tm, tn):
    T, D = base.shape
    return pl.pallas_call(
        functools.partial(_final_kernel, tn=tn),
        grid=(T // tm,),
        in_specs=[
            pl.BlockSpec((tm * TOP_K,), lambda i: (i,), memory_space=pltpu.SMEM),
            pl.BlockSpec((tm, D), lambda i: (i, 0)),
            pl.BlockSpec(memory_space=pl.ANY),
            pl.BlockSpec((tm, LANES), lambda i: (i, 0)),
            pl.BlockSpec((1, D), lambda i: (0, 0)),
            pl.BlockSpec((1, D), lambda i: (0, 0)),
        ],
        out_specs=pl.BlockSpec((tm, D), lambda i: (i, 0)),
        out_shape=jax.ShapeDtypeStruct((T, D), F32),
        scratch_shapes=[pltpu.VMEM((TOP_K, tm, D // 2), U32), pltpu.SemaphoreType.DMA(())],
        compiler_params=_cparams(("arbitrary",)),
        name="final",
    )(pos_flat, base, ys, rt, g, b)


def _tile_sizes(T, seq, D):
    small = seq < 512
    return dict(
        tm_in=min(512, seq), tn_in=min(1024, 8 * D),
        tm_mix=min(256, seq), tm_route=min(256, seq),
        tme=128 if small else 512, tn_e=min(512, D),
        tm_perm=256 if small else 512,
    )


def _encoder_layer(x, p, n_seq, seq, layer, depth, ln_emb_g, ln_emb_b, w_in, lower_bounds, hg_norm_g,
                   w_hg_out, conv_w, conv_b, conv_ln_g, conv_ln_b, w_conv_out, w_out, ln1_g, ln1_b,
                   w_router, b_router, w_exp1, b_exp1, w_exp2, b_exp2, w_ple_gate, w_ple_proj,
                   ln2_g, ln2_b):
    T, D = x.shape
    E = w_router.shape[1]
    H = D // LANES
    ts = _tile_sizes(T, seq, D)
    alpha = (2.0 * depth) ** 0.25
    row = lambda v: v.reshape(1, -1)

    P = _inproj(x, row(ln_emb_g), row(ln_emb_b), w_in.astype(BF16), ts["tm_in"], ts["tn_in"])
    lbw = lower_bounds.reshape(2, lower_bounds.shape[1], H, 1, LANES).transpose(0, 2, 1, 3, 4)
    o_f, o_b = _scan(P, lbw, n_seq, seq, layer)
    small = (row(hg_norm_g), conv_w, row(conv_b), row(conv_ln_g), row(conv_ln_b),
             row(ln_emb_g), row(ln_emb_b), row(ln1_g), row(ln1_b))
    x1 = _mix(o_f, o_b, P, x, w_hg_out.astype(BF16), w_conv_out.astype(BF16), w_out.astype(BF16),
              small, seq, ts["tm_mix"], alpha)
    base, xp, rt, counts = _route(x1, p, w_ple_gate.astype(BF16), w_ple_proj.astype(BF16),
                                  w_router, row(b_router), ts["tm_route"], alpha)

    tme = ts["tme"]
    n_tiles = (T * TOP_K) // tme + E
    cnt = counts[0].astype(I32)
    padded = ((cnt + tme - 1) // tme) * tme
    ends = jnp.cumsum(padded)
    starts = ends - padded
    idx = rt[:, :TOP_K].astype(I32)
    rank = rt[:, 2 * TOP_K:3 * TOP_K].astype(I32)
    pos = (starts[idx] + rank).reshape(-1)
    tile_start = jnp.arange(n_tiles, dtype=I32) * tme
    tile_e = jnp.minimum(jnp.searchsorted(ends, tile_start, side="right"), E - 1).astype(I32)
    tile_v = (tile_start < ends[-1]).astype(I32)

    xs = _scatter_rows(pos, xp, n_tiles * tme, ts["tm_perm"])
    hs = _gm1(tile_e, tile_v, xs, w_exp1, b_exp1.reshape(E, 1, -1), tme, ts["tn_e"])
    ys = _gm2(tile_e, tile_v, hs, w_exp2, b_exp2.reshape(E, 1, -1), tme, ts["tn_e"])
    return _final(pos, base, ys, rt, row(ln2_g), row(ln2_b), ts["tm_perm"], ts["tn_e"])


def kernel(x_prompt, x_sample, p_prompt, p_sample, ln_emb_g, ln_emb_b, w_in, lower_bounds, hg_norm_g, w_hg_out, conv_w, conv_b, conv_ln_g, conv_ln_b, w_conv_out, w_out, ln1_g, ln1_b, w_router, b_router, w_exp1, b_exp1, w_exp2, b_exp2, w_ple_gate, w_ple_proj, ln2_g, ln2_b):
    depth = w_in.shape[0]
    assert depth == 1, "the per-layer input embeddings and weights are wired for a single layer"
    D = x_prompt.shape[-1]
    seq = x_prompt.shape[1]
    assert x_sample.shape[1] == seq
    nb_p, nb_s = x_prompt.shape[0], x_sample.shape[0]
    x = jnp.concatenate([x_prompt.reshape(-1, D), x_sample.reshape(-1, D)], axis=0)
    p = jnp.concatenate([p_prompt[0].reshape(-1, p_prompt.shape[-1]),
                         p_sample[0].reshape(-1, p_sample.shape[-1])], axis=0)
    y = _encoder_layer(x, p, nb_p + nb_s, seq, 0, depth, ln_emb_g, ln_emb_b, w_in[0], lower_bounds,
                       hg_norm_g[0], w_hg_out[0], conv_w[0], conv_b[0], conv_ln_g[0], conv_ln_b[0],
                       w_conv_out[0], w_out[0], ln1_g[0], ln1_b[0], w_router[0], b_router[0],
                       w_exp1[0], b_exp1[0], w_exp2[0], b_exp2[0], w_ple_gate[0], w_ple_proj[0],
                       ln2_g[0], ln2_b[0])
    n_p = nb_p * seq
    return (y[:n_p].reshape(x_prompt.shape), y[n_p:].reshape(x_sample.shape))
```

```python
import functools
import math

import jax
import jax.numpy as jnp
from jax import lax
from jax.experimental import pallas as pl
from jax.experimental.pallas import tpu as pltpu

F32 = jnp.float32
BF16 = jnp.bfloat16
U32 = jnp.uint32
I32 = jnp.int32

LANES = 128
NORM_EPS = 1e-5
CONV_K = 31
TOP_K = 4
SWIGLU_ALPHA = 1.702
SWIGLU_LIMIT = 7.0
SCAN_CHUNK = 64
SCAN_GROUP = 4
DIAG = 8
MAX_LOG_DECAY = 80.0
VMEM_LIMIT = 56 * 1024 * 1024


def _cparams(sem, vmem=VMEM_LIMIT):
    return pltpu.CompilerParams(dimension_semantics=sem, vmem_limit_bytes=vmem)


def _resident(shape):
    nd = len(shape)
    return pl.BlockSpec(shape, lambda *_: (0,) * nd, pipeline_mode=pl.Buffered(1))


def _ln(x, g, b):
    mu = jnp.mean(x, axis=-1, keepdims=True)
    xc = x - mu
    var = jnp.mean(xc * xc, axis=-1, keepdims=True)
    return xc * lax.rsqrt(var + NORM_EPS) * g + b


def _sigmoid(x):
    return 1.0 / (1.0 + jnp.exp(-x))


def _pack_bf16_pair(hi, lo):
    hb = pltpu.bitcast(hi.astype(BF16).astype(F32), U32)
    lb = pltpu.bitcast(lo.astype(BF16).astype(F32), U32)
    return (hb & jnp.uint32(0xFFFF0000)) | (lb >> 16)


def _unpack_bf16_pair(w):
    hi = pltpu.bitcast(w & jnp.uint32(0xFFFF0000), F32)
    lo = pltpu.bitcast(w << 16, F32)
    return hi, lo


def _store_token_rows(ref, hi, lo):
    tm, W = hi.shape
    rw = W // LANES
    packed = _pack_bf16_pair(hi, lo)
    for c in range(rw):
        ref[pl.ds(c, tm, stride=rw), :] = packed[:, c * LANES:(c + 1) * LANES]


def _load_token_rows(ref, tm):
    rw = ref.shape[0] // tm
    pairs = [_unpack_bf16_pair(ref[pl.ds(c, tm, stride=rw), :]) for c in range(rw)]
    return jnp.concatenate([h for h, _ in pairs] + [l for _, l in pairs], axis=-1)


def _inproj_kernel(x_ref, g_ref, b_ref, w_ref, o_ref, h_ref):
    @pl.when(pl.program_id(1) == 0)
    def _():
        h_ref[...] = _ln(x_ref[...], g_ref[...], b_ref[...]).astype(BF16)

    acc = jnp.dot(h_ref[...], w_ref[...], preferred_element_type=F32)
    for c in range(o_ref.shape[0]):
        o_ref[c] = acc[:, c * LANES:(c + 1) * LANES].astype(BF16)


def _inproj(x, g, b, w_bf, tm, tn):
    T, D = x.shape
    N = w_bf.shape[1]
    return pl.pallas_call(
        _inproj_kernel,
        grid=(T // tm, N // tn),
        in_specs=[
            pl.BlockSpec((tm, D), lambda i, j: (i, 0)),
            pl.BlockSpec((1, D), lambda i, j: (0, 0)),
            pl.BlockSpec((1, D), lambda i, j: (0, 0)),
            pl.BlockSpec((D, tn), lambda i, j: (0, j)),
        ],
        out_specs=pl.BlockSpec((tn // LANES, tm, LANES), lambda i, j: (j, i, 0)),
        out_shape=jax.ShapeDtypeStruct((N // LANES, T, LANES), BF16),
        scratch_shapes=[pltpu.VMEM((tm, D), BF16)],
        compiler_params=_cparams(("parallel", "arbitrary")),
        name="inproj",
    )(x, g, b, w_bf)


def _cumsum_rows(g, tri):
    g1 = g.astype(BF16)
    r1 = g - g1.astype(F32)
    g2 = r1.astype(BF16)
    g3 = (r1 - g2.astype(F32)).astype(BF16)
    out = jnp.dot(tri, g1, preferred_element_type=F32)
    out = out + jnp.dot(tri, g2, preferred_element_type=F32)
    return out + jnp.dot(tri, g3, preferred_element_type=F32)


def _scan_chunk(q, v_bf, vt_bf, fl, lb, st, rev):
    C = q.shape[0]
    gate = lb + (1.0 - lb) * _sigmoid(fl)
    k = 1.0 - gate
    g = jnp.log(gate)
    row = lax.broadcasted_iota(I32, (C, C), 0)
    col = lax.broadcasted_iota(I32, (C, C), 1)
    tri = (col >= row) if rev else (col <= row)
    b = _cumsum_rows(g, jnp.where(tri, 1.0, 0.0).astype(BF16))
    b_end = b[0:1, :] if rev else b[C - 1:C, :]
    qe = (q * jnp.exp(b)).astype(BF16)

    scores = _scores_unbounded(q, k, gate, b, row, col, rev)
    kdec = (k * jnp.exp(b_end - b)).astype(BF16)
    o = jnp.dot(scores.astype(BF16), v_bf, preferred_element_type=F32)
    o = o + lax.dot_general(qe, st.astype(BF16), (((1,), (1,)), ((), ())), preferred_element_type=F32)
    st_new = st * jnp.exp(b_end) + jnp.dot(vt_bf, kdec, preferred_element_type=F32)
    return o, st_new


def _scores_unbounded(q, k, gate, b, row, col, rev):
    C = q.shape[0]
    scores = jnp.zeros((C, C), F32)
    m = C // 2
    while m >= DIAG:
        n = C // m
        bm = b.reshape(n, m, b.shape[-1])
        zero = jnp.zeros((1, 1, b.shape[-1]), F32)
        if rev:
            edge = bm[:, 0:1, :]
            ref_q = jnp.concatenate([edge[1:], zero], axis=0)
        else:
            edge = bm[:, m - 1:m, :]
            ref_q = jnp.concatenate([zero, edge[:-1]], axis=0)
        eq = (bm - ref_q).reshape(C, -1)
        ek = (edge - bm).reshape(C, -1)
        ql = (q * jnp.exp(eq)).astype(BF16)
        kl = (k * jnp.exp(ek)).astype(BF16)
        s_l = lax.dot_general(ql, kl, (((1,), (1,)), ((), ())), preferred_element_type=F32)
        rb, cb = row // m, col // m
        if rev:
            mask = (cb == rb + 1) & ((rb & 1) == 0)
        else:
            mask = (rb == cb + 1) & ((cb & 1) == 0)
        scores = scores + jnp.where(mask, s_l, 0.0)
        m //= 2

    rin = row & (DIAG - 1)
    gprod = None
    for d in range(DIAG):
        if d == 0:
            a = q * k
            mask = row == col
        else:
            sh = (C - d) if rev else d
            gprod = gate if d == 1 else gate * pltpu.roll(gprod, (C - 1) if rev else 1, axis=0)
            a = q * pltpu.roll(k, sh, axis=0) * gprod
            if rev:
                mask = (col == row + d) & (rin + d < DIAG)
            else:
                mask = (col == row - d) & (rin >= d)
        scores = scores + jnp.where(mask, jnp.sum(a, axis=-1, keepdims=True), 0.0)
    return scores


def _scan_group(q4, v4, f4, lb4, st4, rev):
    G, C, W = q4.shape
    R = G * C
    stack = lambda t: t.reshape(R, W)
    qr = stack(q4.astype(F32))
    q = qr * _sigmoid(qr)
    gate = lb4 + (1.0 - lb4) * _sigmoid(f4.astype(F32))
    k = stack(1.0 - gate)
    g = jnp.log(gate)

    g1 = g.astype(BF16)
    r1 = g - g1.astype(F32)
    g2 = r1.astype(BF16)
    g3 = (r1 - g2.astype(F32)).astype(BF16)
    wide = lambda t: jnp.concatenate([t[h] for h in range(G)], axis=-1)
    g_split = jnp.concatenate([wide(g1), wide(g2), wide(g3)], axis=0)
    t_row = lax.broadcasted_iota(I32, (C, 3 * C), 0)
    t_col = lax.broadcasted_iota(I32, (C, 3 * C), 1) & (C - 1)
    tri3 = jnp.where((t_col >= t_row) if rev else (t_col <= t_row), 1.0, 0.0).astype(BF16)
    b_wide = jnp.dot(tri3, g_split, preferred_element_type=F32)
    b = jnp.concatenate([b_wide[:, h * W:(h + 1) * W] for h in range(G)], axis=0)

    qe = (q * jnp.exp(b)).astype(BF16)
    kt = k * jnp.exp(-b)
    s_all = lax.dot_general(qe, kt.astype(BF16), (((1,), (1,)), ((), ())), preferred_element_type=F32)
    row = lax.broadcasted_iota(I32, (R, R), 0)
    col = lax.broadcasted_iota(I32, (R, R), 1)
    order = (col >= row) if rev else (col <= row)
    scores = jnp.where(order & ((row // C) == (col // C)), s_all, 0.0).astype(BF16)
    o = jnp.dot(scores, stack(v4), preferred_element_type=F32)

    q_head = lax.broadcasted_iota(I32, (R, W), 0) // C
    q_blk = jnp.concatenate([jnp.where(q_head == h, qe, jnp.zeros_like(qe)) for h in range(G)], axis=-1)
    st_cat = jnp.concatenate([st4[h].astype(BF16) for h in range(G)], axis=-1)
    o = o + lax.dot_general(q_blk, st_cat, (((1,), (1,)), ((), ())), preferred_element_type=F32)

    b3 = b.reshape(G, C, W)
    dec = jnp.exp(b3[:, 0:1, :] if rev else b3[:, C - 1:C, :])
    kdec = stack(kt.reshape(G, C, W) * dec).astype(BF16)
    vt = stack(v4.astype(F32)).T.astype(BF16)
    v_head = lax.broadcasted_iota(I32, (W, R), 1) // C
    vt_blk = jnp.concatenate([jnp.where(v_head == h, vt, jnp.zeros_like(vt)) for h in range(G)], axis=0)
    upd = jnp.dot(vt_blk, kdec, preferred_element_type=F32)
    return o.reshape(G, C, W), st4 * dec + upd.reshape(G, W, W)


def _scan_kernel(qf_ref, vf_ref, ff_ref, qb_ref, vb_ref, fb_ref, lbw_ref, of_ref, ob_ref, sf_ref, sb_ref,
                 *, layer):
    @pl.when(pl.program_id(1) == 0)
    def _():
        sf_ref[...] = jnp.zeros_like(sf_ref)
        sb_ref[...] = jnp.zeros_like(sb_ref)

    def lower_bounds(raw):
        e = jnp.exp(raw - jnp.max(raw, axis=-3, keepdims=True))
        return jnp.sum(e[..., :layer + 1, :, :], axis=-3) / jnp.sum(e, axis=-3)

    dirs = ((qf_ref, vf_ref, ff_ref, of_ref, sf_ref), (qb_ref, vb_ref, fb_ref, ob_ref, sb_ref))
    H = qf_ref.shape[0]
    G = math.gcd(H, SCAN_GROUP)

    def run_bounded():
        def body(gi, carry):
            hs = pl.ds(pl.multiple_of(gi * G, G), G)
            for rev, (q_ref, v_ref, f_ref, o_ref, s_ref) in enumerate(dirs):
                o, st = _scan_group(q_ref[hs], v_ref[hs], f_ref[hs], lower_bounds(lbw_ref[rev, hs]),
                                    s_ref[hs], bool(rev))
                o_ref[hs] = o.astype(BF16)
                s_ref[hs] = st
            return carry

        lax.fori_loop(0, H // G, body, 0, unroll=2)

    def run_unbounded():
        def body(h, carry):
            for rev, (q_ref, v_ref, f_ref, o_ref, s_ref) in enumerate(dirs):
                qr = q_ref[h].astype(F32)
                q = qr * _sigmoid(qr)
                v = v_ref[h].astype(F32)
                o, st = _scan_chunk(q, v.astype(BF16), v.T.astype(BF16), f_ref[h].astype(F32),
                                    lower_bounds(lbw_ref[rev, h]), s_ref[h], bool(rev))
                o_ref[h] = o.astype(BF16)
                s_ref[h] = st
            return carry

        lax.fori_loop(0, H, body, 0)

    lb_all = lower_bounds(lbw_ref[...])
    lb_min = jnp.min(jnp.min(lb_all, axis=(0, 1)))
    is_bounded = lb_min > math.exp(-MAX_LOG_DECAY / SCAN_CHUNK)
    pl.when(is_bounded)(run_bounded)
    pl.when(jnp.logical_not(is_bounded))(run_unbounded)


def _scan(P, lbw, n_seq, seq, layer):
    H = lbw.shape[1]
    T = P.shape[1]
    C = SCAN_CHUNK
    nc = seq // C

    def spec(sec, rev):
        if rev:
            return pl.BlockSpec((H, C, LANES), lambda s, c: (sec, s * nc + nc - 1 - c, 0))
        return pl.BlockSpec((H, C, LANES), lambda s, c: (sec, s * nc + c, 0))

    out_sds = jax.ShapeDtypeStruct((H, T, LANES), BF16)
    return pl.pallas_call(
        functools.partial(_scan_kernel, layer=layer),
        grid=(n_seq, nc),
        in_specs=[spec(0, False), spec(3, False), spec(1, False),
                  spec(0, True), spec(3, True), spec(2, True),
                  pl.BlockSpec(lbw.shape, lambda s, c: (0,) * lbw.ndim)],
        out_specs=[spec(0, False), spec(0, True)],
        out_shape=[out_sds, out_sds],
        scratch_shapes=[pltpu.VMEM((H, LANES, LANES), F32), pltpu.VMEM((H, LANES, LANES), F32)],
        compiler_params=_cparams(("parallel", "arbitrary")),
        name="scan",
    )(P, P, P, P, P, P, lbw)


def _mix_kernel(of_ref, ob_ref, og_ref, glu_ref, gprev_ref, gnext_ref, ga_ref, gc_ref, x_ref,
                whg_ref, wcv_ref, wout_ref, hgn_ref, cw_ref, cb_ref, clg_ref, clb_ref,
                eg_ref, eb_ref, l1g_ref, l1b_ref, x1_ref, hc_ref, cv_ref, *, tiles_per_seq, alpha):
    H, tm, _ = of_ref.shape
    nch = glu_ref.shape[0] // 2
    halo = gprev_ref.shape[1]
    i = pl.program_id(0)
    first = (i % tiles_per_seq) == 0
    last = (i % tiles_per_seq) == tiles_per_seq - 1

    def cat(ref, lo=0, hi=None):
        hi = ref.shape[0] if hi is None else hi
        return jnp.concatenate([ref[c].astype(F32) for c in range(lo, hi)], axis=-1)

    o = cat(of_ref) + cat(ob_ref)
    og = cat(og_ref)
    on = o * lax.rsqrt(jnp.mean(o * o, axis=-1, keepdims=True) + NORM_EPS) * hgn_ref[...]
    a = jnp.dot((on * (og * _sigmoid(og))).astype(BF16), whg_ref[...], preferred_element_type=F32)

    def glu(ref):
        return cat(ref, 0, nch) * _sigmoid(cat(ref, nch, 2 * nch))

    hc_ref[pl.ds(0, halo), :] = jnp.where(first, 0.0, glu(gprev_ref))
    hc_ref[pl.ds(halo, tm), :] = glu(glu_ref)
    hc_ref[pl.ds(halo + tm, halo), :] = jnp.where(last, 0.0, glu(gnext_ref))
    rb = 32
    for c in range(nch):
        cs = slice(c * LANES, (c + 1) * LANES)
        taps = [cw_ref[j:j + 1, cs] for j in range(CONV_K)]
        for r0 in range(0, tm, rb):
            acc = jnp.zeros((rb, LANES), F32)
            for j in range(CONV_K):
                lo = r0 + halo - CONV_K // 2 + j
                acc = acc + taps[j] * hc_ref[lo:lo + rb, cs]
            cv_ref[r0:r0 + rb, cs] = acc
    cn = _ln(cv_ref[...] + cb_ref[...], clg_ref[...], clb_ref[...])
    cc = jnp.dot((cn * _sigmoid(cn)).astype(BF16), wcv_ref[...], preferred_element_type=F32)

    mixed = jnp.dot((_sigmoid(cat(ga_ref)) * a + _sigmoid(cat(gc_ref)) * cc).astype(BF16), wout_ref[...],
                    preferred_element_type=F32)
    h = _ln(x_ref[...], eg_ref[...], eb_ref[...])
    x1_ref[...] = _ln(alpha * h + mixed, l1g_ref[...], l1b_ref[...])


def _mix(o_f, o_b, P, x, whg, wcv, wout, small, seq, tm, alpha):
    H, T, _ = o_f.shape
    D = x.shape[1]
    Cw = wcv.shape[0]
    halo = 16
    nt = T // tm
    hb = tm // halo
    last_hb = T // halo - 1
    sec = lambda s: pl.BlockSpec((H, tm, LANES), lambda i: (s, i, 0))
    vec = lambda n: pl.BlockSpec((1, n), lambda i: (0, 0))
    in_specs = [
        sec(0), sec(0), sec(4), sec(5),
        pl.BlockSpec((H, halo, LANES), lambda i: (5, jnp.maximum(i * hb - 1, 0), 0)),
        pl.BlockSpec((H, halo, LANES), lambda i: (5, jnp.minimum((i + 1) * hb, last_hb), 0)),
        sec(6), sec(7),
        pl.BlockSpec((tm, D), lambda i: (i, 0)),
        _resident(whg.shape), _resident(wcv.shape), _resident(wout.shape),
        vec(D), pl.BlockSpec((CONV_K, Cw), lambda i: (0, 0)), vec(Cw), vec(Cw), vec(Cw),
        vec(D), vec(D), vec(D), vec(D),
    ]
    return pl.pallas_call(
        functools.partial(_mix_kernel, tiles_per_seq=seq // tm, alpha=alpha),
        grid=(nt,),
        in_specs=in_specs,
        out_specs=pl.BlockSpec((tm, D), lambda i: (i, 0)),
        out_shape=jax.ShapeDtypeStruct((T, D), F32),
        scratch_shapes=[pltpu.VMEM((tm + 2 * halo, Cw), F32), pltpu.VMEM((tm, Cw), F32)],
        compiler_params=_cparams(("parallel",)),
        name="mix",
    )(o_f, o_b, P, P, P, P, P, P, x, whg, wcv, wout, *small)


def _route_kernel(x1_ref, p_ref, wpg_ref, wpp_ref, wr_ref, br_ref, base_ref, xp_ref, rt_ref, cnt_ref,
                  *, alpha):
    tm, D = x1_ref.shape
    E = wr_ref.shape[1]

    @pl.when(pl.program_id(0) == 0)
    def _():
        cnt_ref[...] = jnp.zeros_like(cnt_ref)

    x1 = x1_ref[...]
    xb = x1.astype(BF16)
    gate = _sigmoid(jnp.dot(xb, wpg_ref[...], preferred_element_type=F32))
    ple = gate * jnp.dot(p_ref[...].astype(BF16), wpp_ref[...], preferred_element_type=F32)
    base_ref[...] = alpha * x1 + ple
    _store_token_rows(xp_ref, x1[:, :D // 2], x1[:, D // 2:])

    logits = jnp.dot(x1, wr_ref[...], preferred_element_type=F32,
                     precision=lax.Precision.HIGHEST) + br_ref[...]
    col = lax.broadcasted_iota(I32, (tm, E), 1).astype(F32)
    lane = lax.broadcasted_iota(I32, (tm, LANES), 1)
    sels, vals = [], []
    work = logits
    for _ in range(TOP_K):
        mx = jnp.max(work, axis=-1, keepdims=True)
        idx = jnp.min(jnp.where(work == mx, col, float(E)), axis=-1, keepdims=True)
        sel = col == idx
        sels.append((sel, idx))
        vals.append(mx)
        work = jnp.where(sel, -jnp.inf, work)
    pf = sum(jnp.where(sel, 1.0, 0.0) for sel, _ in sels)
    r = lax.broadcasted_iota(I32, (tm, tm), 0)
    c = lax.broadcasted_iota(I32, (tm, tm), 1)
    before = jnp.dot(jnp.where(c < r, 1.0, 0.0).astype(BF16), pf.astype(BF16),
                     preferred_element_type=F32) + cnt_ref[...]
    es = [jnp.exp(v - vals[0]) for v in vals]
    den = es[0] + es[1] + es[2] + es[3]
    out = jnp.zeros((tm, LANES), F32)
    for kk in range(TOP_K):
        sel, idx = sels[kk]
        rank = jnp.sum(jnp.where(sel, before, 0.0), axis=-1, keepdims=True)
        out = jnp.where(lane == kk, idx, out)
        out = jnp.where(lane == TOP_K + kk, es[kk] / den, out)
        out = jnp.where(lane == 2 * TOP_K + kk, rank, out)
    rt_ref[...] = out
    cnt_ref[...] = cnt_ref[...] + jnp.sum(pf, axis=0, keepdims=True)


def _route(x1, p, wpg, wpp, wr, br, tm, alpha):
    T, D = x1.shape
    E = wr.shape[1]
    return pl.pallas_call(
        functools.partial(_route_kernel, alpha=alpha),
        grid=(T // tm,),
        in_specs=[
            pl.BlockSpec((tm, D), lambda i: (i, 0)),
            pl.BlockSpec((tm, p.shape[1]), lambda i: (i, 0)),
            _resident(wpg.shape), _resident(wpp.shape),
            pl.BlockSpec(wr.shape, lambda i: (0, 0)),
            pl.BlockSpec((1, E), lambda i: (0, 0)),
        ],
        out_specs=[
            pl.BlockSpec((tm, D), lambda i: (i, 0)),
            pl.BlockSpec((tm * (D // 2) // LANES, LANES), lambda i: (i, 0)),
            pl.BlockSpec((tm, LANES), lambda i: (i, 0)),
            pl.BlockSpec((1, E), lambda i: (0, 0)),
        ],
        out_shape=[
            jax.ShapeDtypeStruct((T, D), F32),
            jax.ShapeDtypeStruct((T * (D // 2) // LANES, LANES), U32),
            jax.ShapeDtypeStruct((T, LANES), F32),
            jax.ShapeDtypeStruct((1, E), F32),
        ],
        compiler_params=_cparams(("arbitrary",)),
        name="route",
    )(x1, p, wpg, wpp, wr, br)


def _token_copy(src, s, dst, d, sem, rw):
    return pltpu.make_async_copy(src.at[pl.ds(pl.multiple_of(s * rw, rw), rw), :],
                                 dst.at[pl.ds(pl.multiple_of(d * rw, rw), rw), :], sem)


def _wait_tokens(hbm, vmem, sem):
    pltpu.make_async_copy(hbm.at[pl.ds(0, vmem.shape[0]), :], vmem, sem).wait()


def _scatter_kernel(pos_ref, src_ref, init_ref, dst_ref, sem, *, rw):
    del init_ref
    tm = src_ref.shape[0] // rw

    def issue(t, carry):
        for kk in range(TOP_K):
            _token_copy(src_ref, t, dst_ref, pos_ref[t * TOP_K + kk], sem, rw).start()
        return carry

    lax.fori_loop(0, tm, issue, 0)
    for _ in range(TOP_K):
        _wait_tokens(dst_ref, src_ref, sem)


def _scatter_rows(pos_flat, src, n_dst, tm, rw):
    init = jnp.zeros((n_dst * rw, LANES), src.dtype)
    return pl.pallas_call(
        functools.partial(_scatter_kernel, rw=rw),
        grid=(src.shape[0] // (tm * rw),),
        in_specs=[
            pl.BlockSpec((tm * TOP_K,), lambda i: (i,), memory_space=pltpu.SMEM),
            pl.BlockSpec((tm * rw, LANES), lambda i: (i, 0)),
            pl.BlockSpec(memory_space=pl.ANY),
        ],
        out_specs=pl.BlockSpec(memory_space=pl.ANY),
        out_shape=jax.ShapeDtypeStruct(init.shape, init.dtype),
        scratch_shapes=[pltpu.SemaphoreType.DMA(())],
        input_output_aliases={2: 0},
        compiler_params=pltpu.CompilerParams(dimension_semantics=("arbitrary",), has_side_effects=True),
        name="scatter_rows",
    )(pos_flat, src, init)


def _weights_changed(te_ref):
    m = pl.program_id(1)
    return (m == 0) | (te_ref[m] != te_ref[jnp.maximum(m - 1, 0)])


def _gm1_kernel(te_ref, tv_ref, xs_ref, wg_ref, wl_ref, bg_ref, bl_ref, h_ref, wgb_ref, wlb_ref):
    m = pl.program_id(1)

    @pl.when(_weights_changed(te_ref))
    def _():
        wgb_ref[...] = wg_ref[...].astype(BF16)
        wlb_ref[...] = wl_ref[...].astype(BF16)

    @pl.when(tv_ref[m] != 0)
    def _():
        x = _load_token_rows(xs_ref, h_ref.shape[0]).astype(BF16)
        hg = jnp.dot(x, wgb_ref[...], preferred_element_type=F32) + bg_ref[...]
        hl = jnp.dot(x, wlb_ref[...], preferred_element_type=F32) + bl_ref[...]
        hg = jnp.minimum(hg, SWIGLU_LIMIT)
        hl = jnp.clip(hl, -SWIGLU_LIMIT, SWIGLU_LIMIT)
        h_ref[...] = (hg * _sigmoid(SWIGLU_ALPHA * hg) * (hl + 1.0)).astype(BF16)

    @pl.when(tv_ref[m] == 0)
    def _():
        h_ref[...] = jnp.zeros_like(h_ref)


def _gm1(tile_e, tile_v, xs, w1, b1, tme, tn):
    E, D, F2 = w1.shape
    rw = D // 2 // LANES
    Np = xs.shape[0] // rw
    Fh = F2 // 2
    nj = Fh // tn
    grid_spec = pltpu.PrefetchScalarGridSpec(
        num_scalar_prefetch=2,
        grid=(nj, Np // tme),
        in_specs=[
            pl.BlockSpec((tme * rw, LANES), lambda j, m, te, tv: (m, 0)),
            pl.BlockSpec((None, D, tn), lambda j, m, te, tv: (te[m], 0, j)),
            pl.BlockSpec((None, D, tn), lambda j, m, te, tv: (te[m], 0, nj + j)),
            pl.BlockSpec((None, 1, tn), lambda j, m, te, tv: (te[m], 0, j)),
            pl.BlockSpec((None, 1, tn), lambda j, m, te, tv: (te[m], 0, nj + j)),
        ],
        out_specs=pl.BlockSpec((tme, tn), lambda j, m, te, tv: (m, j)),
        scratch_shapes=[pltpu.VMEM((D, tn), BF16), pltpu.VMEM((D, tn), BF16)],
    )
    return pl.pallas_call(
        _gm1_kernel,
        grid_spec=grid_spec,
        out_shape=jax.ShapeDtypeStruct((Np, Fh), BF16),
        compiler_params=_cparams(("arbitrary", "arbitrary")),
        name="gm1",
    )(tile_e, tile_v, xs, w1, w1, b1, b1)


def _gm2_kernel(te_ref, tv_ref, h_ref, w_ref, b_ref, y_ref, wb_ref):
    m = pl.program_id(1)
    half = w_ref.shape[1] // 2

    @pl.when(_weights_changed(te_ref))
    def _():
        wb_ref[...] = w_ref[...].astype(BF16)

    @pl.when(tv_ref[m] != 0)
    def _():
        y = jnp.dot(h_ref[...], wb_ref[...], preferred_element_type=F32) + b_ref[...]
        _store_token_rows(y_ref, y[:, :half], y[:, half:])

    @pl.when(tv_ref[m] == 0)
    def _():
        y_ref[...] = jnp.zeros_like(y_ref)


def _gm2(tile_e, tile_v, hs, w2, b2, tme):
    Np, Fh = hs.shape
    E, _, D = w2.shape
    rw = D // 2 // LANES
    grid_spec = pltpu.PrefetchScalarGridSpec(
        num_scalar_prefetch=2,
        grid=(1, Np // tme),
        in_specs=[
            pl.BlockSpec((tme, Fh), lambda j, m, te, tv: (m, 0)),
            pl.BlockSpec((None, Fh, D), lambda j, m, te, tv: (te[m], 0, 0), pipeline_mode=pl.Buffered(1)),
            pl.BlockSpec((None, 1, D), lambda j, m, te, tv: (te[m], 0, 0)),
        ],
        out_specs=pl.BlockSpec((tme * rw, LANES), lambda j, m, te, tv: (m, 0)),
        scratch_shapes=[pltpu.VMEM((Fh, D), BF16)],
    )
    return pl.pallas_call(
        _gm2_kernel,
        grid_spec=grid_spec,
        out_shape=jax.ShapeDtypeStruct((Np * rw, LANES), U32),
        compiler_params=_cparams(("arbitrary", "arbitrary")),
        name="gm2",
    )(tile_e, tile_v, hs, w2, b2)


def _final_kernel(pos_ref, base_ref, ys_ref, rt_ref, g_ref, b_ref, o_ref, y_ref, sem):
    tm = base_ref.shape[0]
    rw = y_ref.shape[1] // tm

    def issue(t, carry):
        for kk in range(TOP_K):
            _token_copy(ys_ref, pos_ref[t * TOP_K + kk], y_ref.at[kk], t, sem, rw).start()
        return carry

    lax.fori_loop(0, tm, issue, 0)
    for kk in range(TOP_K):
        _wait_tokens(ys_ref, y_ref.at[kk], sem)

    acc = base_ref[...]
    rt = rt_ref[...]
    for kk in range(TOP_K):
        acc = acc + rt[:, TOP_K + kk:TOP_K + kk + 1] * _load_token_rows(y_ref.at[kk], tm)
    o_ref[...] = _ln(acc, g_ref[...], b_ref[...])


def _final(pos_flat, base, ys, rt, g, b, tm):
    T, D = base.shape
    rw = D // 2 // LANES
    return pl.pallas_call(
        _final_kernel,
        grid=(T // tm,),
        in_specs=[
            pl.BlockSpec((tm * TOP_K,), lambda i: (i,), memory_space=pltpu.SMEM),
            pl.BlockSpec((tm, D), lambda i: (i, 0)),
            pl.BlockSpec(memory_space=pl.ANY),
            pl.BlockSpec((tm, LANES), lambda i: (i, 0)),
            pl.BlockSpec((1, D), lambda i: (0, 0)),
            pl.BlockSpec((1, D), lambda i: (0, 0)),
        ],
        out_specs=pl.BlockSpec((tm, D), lambda i: (i, 0)),
        out_shape=jax.ShapeDtypeStruct((T, D), F32),
        scratch_shapes=[pltpu.VMEM((TOP_K, tm * rw, LANES), U32), pltpu.SemaphoreType.DMA(())],
        compiler_params=_cparams(("arbitrary",)),
        name="final",
    )(pos_flat, base, ys, rt, g, b)


def _tile_sizes(T, seq, D):
    small = seq < 512
    return dict(
        tm_in=min(512, seq), tn_in=min(1024, 8 * D),
        tm_mix=min(256, seq), tm_route=min(256, seq),
        tme=128 if small else 512, tn_e=min(512, D),
        tm_perm=256 if small else 512,
    )


def _encoder_layer(x, p, n_seq, seq, layer, depth, ln_emb_g, ln_emb_b, w_in, lower_bounds, hg_norm_g,
                   w_hg_out, conv_w, conv_b, conv_ln_g, conv_ln_b, w_conv_out, w_out, ln1_g, ln1_b,
                   w_router, b_router, w_exp1, b_exp1, w_exp2, b_exp2, w_ple_gate, w_ple_proj,
                   ln2_g, ln2_b):
    T, D = x.shape
    E = w_router.shape[1]
    H = D // LANES
    ts = _tile_sizes(T, seq, D)
    alpha = (2.0 * depth) ** 0.25
    row = lambda v: v.reshape(1, -1)

    P = _inproj(x, row(ln_emb_g), row(ln_emb_b), w_in.astype(BF16), ts["tm_in"], ts["tn_in"])
    lbw = lower_bounds.reshape(2, lower_bounds.shape[1], H, 1, LANES).transpose(0, 2, 1, 3, 4)
    o_f, o_b = _scan(P, lbw, n_seq, seq, layer)
    small = (row(hg_norm_g), conv_w, row(conv_b), row(conv_ln_g), row(conv_ln_b),
             row(ln_emb_g), row(ln_emb_b), row(ln1_g), row(ln1_b))
    x1 = _mix(o_f, o_b, P, x, w_hg_out.astype(BF16), w_conv_out.astype(BF16), w_out.astype(BF16),
              small, seq, ts["tm_mix"], alpha)
    base, xp, rt, counts = _route(x1, p, w_ple_gate.astype(BF16), w_ple_proj.astype(BF16),
                                  w_router, row(b_router), ts["tm_route"], alpha)

    tme = ts["tme"]
    n_tiles = (T * TOP_K) // tme + E
    cnt = counts[0].astype(I32)
    padded = ((cnt + tme - 1) // tme) * tme
    ends = jnp.cumsum(padded)
    starts = ends - padded
    idx = rt[:, :TOP_K].astype(I32)
    rank = rt[:, 2 * TOP_K:3 * TOP_K].astype(I32)
    pos = (starts[idx] + rank).reshape(-1)
    tile_start = jnp.arange(n_tiles, dtype=I32) * tme
    tile_e = jnp.minimum(jnp.sum((tile_start[:, None] >= ends[None, :]).astype(I32), axis=1), E - 1)
    tile_v = (tile_start < ends[-1]).astype(I32)

    xs = _scatter_rows(pos, xp, n_tiles * tme, ts["tm_perm"], D // 2 // LANES)
    hs = _gm1(tile_e, tile_v, xs, w_exp1, b_exp1.reshape(E, 1, -1), tme, ts["tn_e"])
    ys = _gm2(tile_e, tile_v, hs, w_exp2, b_exp2.reshape(E, 1, -1), tme)
    return _final(pos, base, ys, rt, row(ln2_g), row(ln2_b), ts["tm_perm"])


def kernel(x_prompt, x_sample, p_prompt, p_sample, ln_emb_g, ln_emb_b, w_in, lower_bounds, hg_norm_g, w_hg_out, conv_w, conv_b, conv_ln_g, conv_ln_b, w_conv_out, w_out, ln1_g, ln1_b, w_router, b_router, w_exp1, b_exp1, w_exp2, b_exp2, w_ple_gate, w_ple_proj, ln2_g, ln2_b):
    depth = w_in.shape[0]
    assert depth == 1, "the per-layer input embeddings and weights are wired for a single layer"
    D = x_prompt.shape[-1]
    seq = x_prompt.shape[1]
    assert x_sample.shape[1] == seq
    nb_p, nb_s = x_prompt.shape[0], x_sample.shape[0]
    x = jnp.concatenate([x_prompt.reshape(-1, D), x_sample.reshape(-1, D)], axis=0)
    p = jnp.concatenate([p_prompt[0].reshape(-1, p_prompt.shape[-1]),
                         p_sample[0].reshape(-1, p_sample.shape[-1])], axis=0)
    y = _encoder_layer(x, p, nb_p + nb_s, seq, 0, depth, ln_emb_g, ln_emb_b, w_in[0], lower_bounds,
                       hg_norm_g[0], w_hg_out[0], conv_w[0], conv_b[0], conv_ln_g[0], conv_ln_b[0],
                       w_conv_out[0], w_out[0], ln1_g[0], ln1_b[0], w_router[0], b_router[0],
                       w_exp1[0], b_exp1[0], w_exp2[0], b_exp2[0], w_ple_gate[0], w_ple_proj[0],
                       ln2_g[0], ln2_b[0])
    n_p = nb_p * seq
    return (y[:n_p].reshape(x_prompt.shape), y[n_p:].reshape(x_sample.shape))
```

```python
import functools
import math

import jax
import jax.numpy as jnp
from jax import lax
from jax.experimental import pallas as pl
from jax.experimental.pallas import tpu as pltpu

F32 = jnp.float32
BF16 = jnp.bfloat16
U32 = jnp.uint32
I32 = jnp.int32

LANES = 128
NORM_EPS = 1e-5
CONV_K = 31
TOP_K = 4
SWIGLU_ALPHA = 1.702
SWIGLU_LIMIT = 7.0
SCAN_CHUNK = 64
SCAN_GROUP = 4
DIAG = 8
MAX_LOG_DECAY = 80.0
GM1_PIECE = 512
VMEM_LIMIT = 56 * 1024 * 1024


def _cparams(sem, vmem=VMEM_LIMIT):
    return pltpu.CompilerParams(dimension_semantics=sem, vmem_limit_bytes=vmem)


def _resident(shape):
    nd = len(shape)
    return pl.BlockSpec(shape, lambda *_: (0,) * nd, pipeline_mode=pl.Buffered(1))


def _ln(x, g, b):
    mu = jnp.mean(x, axis=-1, keepdims=True)
    xc = x - mu
    var = jnp.mean(xc * xc, axis=-1, keepdims=True)
    return xc * lax.rsqrt(var + NORM_EPS) * g + b


def _sigmoid(x):
    return 1.0 / (1.0 + jnp.exp(-x))


def _pack_bf16_pair(hi, lo):
    hb = pltpu.bitcast(hi.astype(BF16).astype(F32), U32)
    lb = pltpu.bitcast(lo.astype(BF16).astype(F32), U32)
    return (hb & jnp.uint32(0xFFFF0000)) | (lb >> 16)


def _unpack_bf16_pair(w):
    hi = pltpu.bitcast(w & jnp.uint32(0xFFFF0000), F32)
    lo = pltpu.bitcast(w << 16, F32)
    return hi, lo


def _store_token_rows(ref, hi, lo):
    tm, W = hi.shape
    rw = W // LANES
    packed = _pack_bf16_pair(hi, lo)
    for c in range(rw):
        ref[pl.ds(c, tm, stride=rw), :] = packed[:, c * LANES:(c + 1) * LANES]


def _load_token_rows(ref, tm):
    rw = ref.shape[0] // tm
    pairs = [_unpack_bf16_pair(ref[pl.ds(c, tm, stride=rw), :]) for c in range(rw)]
    return jnp.concatenate([h for h, _ in pairs] + [l for _, l in pairs], axis=-1)


def _inproj_kernel(x_ref, g_ref, b_ref, w_ref, o_ref, h_ref):
    @pl.when(pl.program_id(1) == 0)
    def _():
        h_ref[...] = _ln(x_ref[...], g_ref[...], b_ref[...]).astype(BF16)

    acc = jnp.dot(h_ref[...], w_ref[...], preferred_element_type=F32)
    for c in range(o_ref.shape[0]):
        o_ref[c] = acc[:, c * LANES:(c + 1) * LANES].astype(BF16)


def _inproj(x, g, b, w_bf, tm, tn):
    T, D = x.shape
    N = w_bf.shape[1]
    return pl.pallas_call(
        _inproj_kernel,
        grid=(T // tm, N // tn),
        in_specs=[
            pl.BlockSpec((tm, D), lambda i, j: (i, 0)),
            pl.BlockSpec((1, D), lambda i, j: (0, 0)),
            pl.BlockSpec((1, D), lambda i, j: (0, 0)),
            pl.BlockSpec((D, tn), lambda i, j: (0, j)),
        ],
        out_specs=pl.BlockSpec((tn // LANES, tm, LANES), lambda i, j: (j, i, 0)),
        out_shape=jax.ShapeDtypeStruct((N // LANES, T, LANES), BF16),
        scratch_shapes=[pltpu.VMEM((tm, D), BF16)],
        compiler_params=_cparams(("parallel", "arbitrary")),
        name="inproj",
    )(x, g, b, w_bf)


def _cumsum_rows(g, tri):
    g1 = g.astype(BF16)
    r1 = g - g1.astype(F32)
    g2 = r1.astype(BF16)
    g3 = (r1 - g2.astype(F32)).astype(BF16)
    out = jnp.dot(tri, g1, preferred_element_type=F32)
    out = out + jnp.dot(tri, g2, preferred_element_type=F32)
    return out + jnp.dot(tri, g3, preferred_element_type=F32)


def _scan_chunk(q, v_bf, vt_bf, fl, lb, st, rev):
    C = q.shape[0]
    gate = lb + (1.0 - lb) * _sigmoid(fl)
    k = 1.0 - gate
    g = jnp.log(gate)
    row = lax.broadcasted_iota(I32, (C, C), 0)
    col = lax.broadcasted_iota(I32, (C, C), 1)
    tri = (col >= row) if rev else (col <= row)
    b = _cumsum_rows(g, jnp.where(tri, 1.0, 0.0).astype(BF16))
    b_end = b[0:1, :] if rev else b[C - 1:C, :]
    qe = (q * jnp.exp(b)).astype(BF16)

    scores = _scores_unbounded(q, k, gate, b, row, col, rev)
    kdec = (k * jnp.exp(b_end - b)).astype(BF16)
    o = jnp.dot(scores.astype(BF16), v_bf, preferred_element_type=F32)
    o = o + lax.dot_general(qe, st.astype(BF16), (((1,), (1,)), ((), ())), preferred_element_type=F32)
    st_new = st * jnp.exp(b_end) + jnp.dot(vt_bf, kdec, preferred_element_type=F32)
    return o, st_new


def _scores_unbounded(q, k, gate, b, row, col, rev):
    C = q.shape[0]
    scores = jnp.zeros((C, C), F32)
    m = C // 2
    while m >= DIAG:
        n = C // m
        bm = b.reshape(n, m, b.shape[-1])
        zero = jnp.zeros((1, 1, b.shape[-1]), F32)
        if rev:
            edge = bm[:, 0:1, :]
            ref_q = jnp.concatenate([edge[1:], zero], axis=0)
        else:
            edge = bm[:, m - 1:m, :]
            ref_q = jnp.concatenate([zero, edge[:-1]], axis=0)
        eq = (bm - ref_q).reshape(C, -1)
        ek = (edge - bm).reshape(C, -1)
        ql = (q * jnp.exp(eq)).astype(BF16)
        kl = (k * jnp.exp(ek)).astype(BF16)
        s_l = lax.dot_general(ql, kl, (((1,), (1,)), ((), ())), preferred_element_type=F32)
        rb, cb = row // m, col // m
        if rev:
            mask = (cb == rb + 1) & ((rb & 1) == 0)
        else:
            mask = (rb == cb + 1) & ((cb & 1) == 0)
        scores = scores + jnp.where(mask, s_l, 0.0)
        m //= 2

    rin = row & (DIAG - 1)
    gprod = None
    for d in range(DIAG):
        if d == 0:
            a = q * k
            mask = row == col
        else:
            sh = (C - d) if rev else d
            gprod = gate if d == 1 else gate * pltpu.roll(gprod, (C - 1) if rev else 1, axis=0)
            a = q * pltpu.roll(k, sh, axis=0) * gprod
            if rev:
                mask = (col == row + d) & (rin + d < DIAG)
            else:
                mask = (col == row - d) & (rin >= d)
        scores = scores + jnp.where(mask, jnp.sum(a, axis=-1, keepdims=True), 0.0)
    return scores


def _scan_group(q4, v4, f4, lb4, st4, rev):
    G, C, W = q4.shape
    R = G * C
    stack = lambda t: t.reshape(R, W)
    qr = stack(q4.astype(F32))
    q = qr * _sigmoid(qr)
    gate = lb4 + (1.0 - lb4) * _sigmoid(f4.astype(F32))
    k = stack(1.0 - gate)
    g = jnp.log(gate)

    g1 = g.astype(BF16)
    r1 = g - g1.astype(F32)
    g2 = r1.astype(BF16)
    g3 = (r1 - g2.astype(F32)).astype(BF16)
    wide = lambda t: jnp.concatenate([t[h] for h in range(G)], axis=-1)
    g_split = jnp.concatenate([wide(g1), wide(g2), wide(g3)], axis=0)
    t_row = lax.broadcasted_iota(I32, (C, 3 * C), 0)
    t_col = lax.broadcasted_iota(I32, (C, 3 * C), 1) & (C - 1)
    tri3 = jnp.where((t_col >= t_row) if rev else (t_col <= t_row), 1.0, 0.0).astype(BF16)
    b_wide = jnp.dot(tri3, g_split, preferred_element_type=F32)
    b = jnp.concatenate([b_wide[:, h * W:(h + 1) * W] for h in range(G)], axis=0)

    qe = (q * jnp.exp(b)).astype(BF16)
    kt = k * jnp.exp(-b)
    s_all = lax.dot_general(qe, kt.astype(BF16), (((1,), (1,)), ((), ())), preferred_element_type=F32)
    row = lax.broadcasted_iota(I32, (R, R), 0)
    col = lax.broadcasted_iota(I32, (R, R), 1)
    order = (col >= row) if rev else (col <= row)
    scores = jnp.where(order & ((row // C) == (col // C)), s_all, 0.0).astype(BF16)
    o = jnp.dot(scores, stack(v4), preferred_element_type=F32)

    q_head = lax.broadcasted_iota(I32, (R, W), 0) // C
    q_blk = jnp.concatenate([jnp.where(q_head == h, qe, jnp.zeros_like(qe)) for h in range(G)], axis=-1)
    st_cat = jnp.concatenate([st4[h].astype(BF16) for h in range(G)], axis=-1)
    o = o + lax.dot_general(q_blk, st_cat, (((1,), (1,)), ((), ())), preferred_element_type=F32)

    b3 = b.reshape(G, C, W)
    dec = jnp.exp(b3[:, 0:1, :] if rev else b3[:, C - 1:C, :])
    kdec = stack(kt.reshape(G, C, W) * dec).astype(BF16)
    vt = stack(v4.astype(F32)).T.astype(BF16)
    v_head = lax.broadcasted_iota(I32, (W, R), 1) // C
    vt_blk = jnp.concatenate([jnp.where(v_head == h, vt, jnp.zeros_like(vt)) for h in range(G)], axis=0)
    upd = jnp.dot(vt_blk, kdec, preferred_element_type=F32)
    return o.reshape(G, C, W), st4 * dec + upd.reshape(G, W, W)


def _scan_kernel(qf_ref, vf_ref, ff_ref, qb_ref, vb_ref, fb_ref, lbw_ref, of_ref, ob_ref, sf_ref, sb_ref,
                 *, layer):
    @pl.when(pl.program_id(1) == 0)
    def _():
        sf_ref[...] = jnp.zeros_like(sf_ref)
        sb_ref[...] = jnp.zeros_like(sb_ref)

    def lower_bounds(raw):
        e = jnp.exp(raw - jnp.max(raw, axis=-3, keepdims=True))
        return jnp.sum(e[..., :layer + 1, :, :], axis=-3) / jnp.sum(e, axis=-3)

    dirs = ((qf_ref, vf_ref, ff_ref, of_ref, sf_ref), (qb_ref, vb_ref, fb_ref, ob_ref, sb_ref))
    H = qf_ref.shape[0]
    G = math.gcd(H, SCAN_GROUP)

    def run_bounded():
        def body(gi, carry):
            hs = pl.ds(pl.multiple_of(gi * G, G), G)
            for rev, (q_ref, v_ref, f_ref, o_ref, s_ref) in enumerate(dirs):
                o, st = _scan_group(q_ref[hs], v_ref[hs], f_ref[hs], lower_bounds(lbw_ref[rev, hs]),
                                    s_ref[hs], bool(rev))
                o_ref[hs] = o.astype(BF16)
                s_ref[hs] = st
            return carry

        lax.fori_loop(0, H // G, body, 0, unroll=2)

    def run_unbounded():
        def body(h, carry):
            for rev, (q_ref, v_ref, f_ref, o_ref, s_ref) in enumerate(dirs):
                qr = q_ref[h].astype(F32)
                q = qr * _sigmoid(qr)
                v = v_ref[h].astype(F32)
                o, st = _scan_chunk(q, v.astype(BF16), v.T.astype(BF16), f_ref[h].astype(F32),
                                    lower_bounds(lbw_ref[rev, h]), s_ref[h], bool(rev))
                o_ref[h] = o.astype(BF16)
                s_ref[h] = st
            return carry

        lax.fori_loop(0, H, body, 0)

    lb_all = lower_bounds(lbw_ref[...])
    lb_min = jnp.min(jnp.min(lb_all, axis=(0, 1)))
    is_bounded = lb_min > math.exp(-MAX_LOG_DECAY / SCAN_CHUNK)
    pl.when(is_bounded)(run_bounded)
    pl.when(jnp.logical_not(is_bounded))(run_unbounded)


def _scan(P, lbw, n_seq, seq, layer):
    H = lbw.shape[1]
    T = P.shape[1]
    C = SCAN_CHUNK
    nc = seq // C

    def spec(sec, rev):
        if rev:
            return pl.BlockSpec((H, C, LANES), lambda s, c: (sec, s * nc + nc - 1 - c, 0))
        return pl.BlockSpec((H, C, LANES), lambda s, c: (sec, s * nc + c, 0))

    out_sds = jax.ShapeDtypeStruct((H, T, LANES), BF16)
    return pl.pallas_call(
        functools.partial(_scan_kernel, layer=layer),
        grid=(n_seq, nc),
        in_specs=[spec(0, False), spec(3, False), spec(1, False),
                  spec(0, True), spec(3, True), spec(2, True),
                  pl.BlockSpec(lbw.shape, lambda s, c: (0,) * lbw.ndim)],
        out_specs=[spec(0, False), spec(0, True)],
        out_shape=[out_sds, out_sds],
        scratch_shapes=[pltpu.VMEM((H, LANES, LANES), F32), pltpu.VMEM((H, LANES, LANES), F32)],
        compiler_params=_cparams(("parallel", "arbitrary")),
        name="scan",
    )(P, P, P, P, P, P, lbw)


def _mix_kernel(of_ref, ob_ref, og_ref, glu_ref, gprev_ref, gnext_ref, ga_ref, gc_ref, x_ref,
                whg_ref, wcv_ref, wout_ref, hgn_ref, cw_ref, cb_ref, clg_ref, clb_ref,
                eg_ref, eb_ref, l1g_ref, l1b_ref, x1_ref, hc_ref, cv_ref, sh_ref, *, tiles_per_seq, alpha):
    H, tm, _ = of_ref.shape
    nch = glu_ref.shape[0] // 2
    halo = gprev_ref.shape[1]
    i = pl.program_id(0)
    first = (i % tiles_per_seq) == 0
    last = (i % tiles_per_seq) == tiles_per_seq - 1

    def cat(ref, lo=0, hi=None):
        hi = ref.shape[0] if hi is None else hi
        return jnp.concatenate([ref[c].astype(F32) for c in range(lo, hi)], axis=-1)

    o = cat(of_ref) + cat(ob_ref)
    og = cat(og_ref)
    on = o * lax.rsqrt(jnp.mean(o * o, axis=-1, keepdims=True) + NORM_EPS) * hgn_ref[...]
    a = jnp.dot((on * (og * _sigmoid(og))).astype(BF16), whg_ref[...], preferred_element_type=F32)

    def glu(ref):
        return cat(ref, 0, nch) * _sigmoid(cat(ref, nch, 2 * nch))

    hc_ref[pl.ds(0, halo), :] = jnp.where(first, 0.0, glu(gprev_ref))
    hc_ref[pl.ds(halo, tm), :] = glu(glu_ref)
    hc_ref[pl.ds(halo + tm, halo), :] = jnp.where(last, 0.0, glu(gnext_ref))
    rb = 32
    sub = 8
    span = sh_ref.shape[1]
    for c in range(nch):
        cs = slice(c * LANES, (c + 1) * LANES)
        taps = [cw_ref[j:j + 1, cs] for j in range(CONV_K)]
        for s in range(1, sub):
            sh_ref[s] = hc_ref[s:s + span, cs]
        for r0 in range(0, tm, rb):
            acc = jnp.zeros((rb, LANES), F32)
            for j in range(CONV_K):
                whole, phase = divmod(halo - CONV_K // 2 + j, sub)
                lo = r0 + sub * whole
                win = hc_ref[lo:lo + rb, cs] if phase == 0 else sh_ref[phase, lo:lo + rb, :]
                acc = acc + taps[j] * win
            cv_ref[r0:r0 + rb, cs] = acc
    cn = _ln(cv_ref[...] + cb_ref[...], clg_ref[...], clb_ref[...])
    cc = jnp.dot((cn * _sigmoid(cn)).astype(BF16), wcv_ref[...], preferred_element_type=F32)

    mixed = jnp.dot((_sigmoid(cat(ga_ref)) * a + _sigmoid(cat(gc_ref)) * cc).astype(BF16), wout_ref[...],
                    preferred_element_type=F32)
    h = _ln(x_ref[...], eg_ref[...], eb_ref[...])
    x1_ref[...] = _ln(alpha * h + mixed, l1g_ref[...], l1b_ref[...])


def _mix(o_f, o_b, P, x, whg, wcv, wout, small, seq, tm, alpha):
    H, T, _ = o_f.shape
    D = x.shape[1]
    Cw = wcv.shape[0]
    halo = 16
    nt = T // tm
    hb = tm // halo
    last_hb = T // halo - 1
    sec = lambda s: pl.BlockSpec((H, tm, LANES), lambda i: (s, i, 0))
    vec = lambda n: pl.BlockSpec((1, n), lambda i: (0, 0))
    in_specs = [
        sec(0), sec(0), sec(4), sec(5),
        pl.BlockSpec((H, halo, LANES), lambda i: (5, jnp.maximum(i * hb - 1, 0), 0)),
        pl.BlockSpec((H, halo, LANES), lambda i: (5, jnp.minimum((i + 1) * hb, last_hb), 0)),
        sec(6), sec(7),
        pl.BlockSpec((tm, D), lambda i: (i, 0)),
        _resident(whg.shape), _resident(wcv.shape), _resident(wout.shape),
        vec(D), pl.BlockSpec((CONV_K, Cw), lambda i: (0, 0)), vec(Cw), vec(Cw), vec(Cw),
        vec(D), vec(D), vec(D), vec(D),
    ]
    return pl.pallas_call(
        functools.partial(_mix_kernel, tiles_per_seq=seq // tm, alpha=alpha),
        grid=(nt,),
        in_specs=in_specs,
        out_specs=pl.BlockSpec((tm, D), lambda i: (i, 0)),
        out_shape=jax.ShapeDtypeStruct((T, D), F32),
        scratch_shapes=[pltpu.VMEM((tm + 2 * halo, Cw), F32), pltpu.VMEM((tm, Cw), F32),
                        pltpu.VMEM((8, tm + 2 * halo - 8, LANES), F32)],
        compiler_params=_cparams(("parallel",)),
        name="mix",
    )(o_f, o_b, P, P, P, P, P, P, x, whg, wcv, wout, *small)


def _route_kernel(x1_ref, p_ref, wpg_ref, wpp_ref, wr_ref, br_ref, base_ref, xp_ref, rt_ref, cnt_ref,
                  *, alpha):
    tm, D = x1_ref.shape
    E = wr_ref.shape[1]

    @pl.when(pl.program_id(0) == 0)
    def _():
        cnt_ref[...] = jnp.zeros_like(cnt_ref)

    x1 = x1_ref[...]
    xb = x1.astype(BF16)
    gate = _sigmoid(jnp.dot(xb, wpg_ref[...], preferred_element_type=F32))
    ple = gate * jnp.dot(p_ref[...].astype(BF16), wpp_ref[...], preferred_element_type=F32)
    base_ref[...] = alpha * x1 + ple
    _store_token_rows(xp_ref, x1[:, :D // 2], x1[:, D // 2:])

    wr = wr_ref[...]
    wr_hi = wr.astype(BF16)
    wr_lo = (wr - wr_hi.astype(F32)).astype(BF16)
    x_lo = (x1 - xb.astype(F32)).astype(BF16)
    logits = (jnp.dot(xb, wr_hi, preferred_element_type=F32)
              + jnp.dot(xb, wr_lo, preferred_element_type=F32)
              + jnp.dot(x_lo, wr_hi, preferred_element_type=F32)) + br_ref[...]
    col = lax.broadcasted_iota(I32, (tm, E), 1).astype(F32)
    lane = lax.broadcasted_iota(I32, (tm, LANES), 1)
    sels, vals = [], []
    work = logits
    for _ in range(TOP_K):
        mx = jnp.max(work, axis=-1, keepdims=True)
        idx = jnp.min(jnp.where(work == mx, col, float(E)), axis=-1, keepdims=True)
        sel = col == idx
        sels.append((sel, idx))
        vals.append(mx)
        work = jnp.where(sel, -jnp.inf, work)
    pf = sum(jnp.where(sel, 1.0, 0.0) for sel, _ in sels)
    r = lax.broadcasted_iota(I32, (tm, tm), 0)
    c = lax.broadcasted_iota(I32, (tm, tm), 1)
    before = jnp.dot(jnp.where(c < r, 1.0, 0.0).astype(BF16), pf.astype(BF16),
                     preferred_element_type=F32) + cnt_ref[...]
    es = [jnp.exp(v - vals[0]) for v in vals]
    den = es[0] + es[1] + es[2] + es[3]
    out = jnp.zeros((tm, LANES), F32)
    for kk in range(TOP_K):
        sel, idx = sels[kk]
        rank = jnp.sum(jnp.where(sel, before, 0.0), axis=-1, keepdims=True)
        out = jnp.where(lane == kk, idx, out)
        out = jnp.where(lane == TOP_K + kk, es[kk] / den, out)
        out = jnp.where(lane == 2 * TOP_K + kk, rank, out)
    rt_ref[...] = out
    cnt_ref[...] = cnt_ref[...] + jnp.sum(pf, axis=0, keepdims=True)


def _route(x1, p, wpg, wpp, wr, br, tm, alpha):
    T, D = x1.shape
    E = wr.shape[1]
    return pl.pallas_call(
        functools.partial(_route_kernel, alpha=alpha),
        grid=(T // tm,),
        in_specs=[
            pl.BlockSpec((tm, D), lambda i: (i, 0)),
            pl.BlockSpec((tm, p.shape[1]), lambda i: (i, 0)),
            _resident(wpg.shape), _resident(wpp.shape),
            pl.BlockSpec(wr.shape, lambda i: (0, 0)),
            pl.BlockSpec((1, E), lambda i: (0, 0)),
        ],
        out_specs=[
            pl.BlockSpec((tm, D), lambda i: (i, 0)),
            pl.BlockSpec((tm * (D // 2) // LANES, LANES), lambda i: (i, 0)),
            pl.BlockSpec((tm, LANES), lambda i: (i, 0)),
            pl.BlockSpec((1, E), lambda i: (0, 0)),
        ],
        out_shape=[
            jax.ShapeDtypeStruct((T, D), F32),
            jax.ShapeDtypeStruct((T * (D // 2) // LANES, LANES), U32),
            jax.ShapeDtypeStruct((T, LANES), F32),
            jax.ShapeDtypeStruct((1, E), F32),
        ],
        compiler_params=_cparams(("arbitrary",)),
        name="route",
    )(x1, p, wpg, wpp, wr, br)


def _token_copy(src, s, dst, d, sem, rw):
    return pltpu.make_async_copy(src.at[pl.ds(pl.multiple_of(s * rw, rw), rw), :],
                                 dst.at[pl.ds(pl.multiple_of(d * rw, rw), rw), :], sem)


def _wait_tokens(hbm, vmem, sem):
    pltpu.make_async_copy(hbm.at[pl.ds(0, vmem.shape[0]), :], vmem, sem).wait()


def _scatter_kernel(pos_ref, src_ref, init_ref, dst_ref, sem, *, rw):
    del init_ref
    tm = src_ref.shape[0] // rw

    def issue(t, carry):
        for kk in range(TOP_K):
            _token_copy(src_ref, t, dst_ref, pos_ref[t * TOP_K + kk], sem, rw).start(priority=kk % 2)
        return carry

    lax.fori_loop(0, tm, issue, 0)
    for _ in range(TOP_K):
        _wait_tokens(dst_ref, src_ref, sem)


def _scatter_rows(pos_flat, src, n_dst, tm, rw):
    init = jnp.zeros((n_dst * rw, LANES), src.dtype)
    return pl.pallas_call(
        functools.partial(_scatter_kernel, rw=rw),
        grid=(src.shape[0] // (tm * rw),),
        in_specs=[
            pl.BlockSpec((tm * TOP_K,), lambda i: (i,), memory_space=pltpu.SMEM),
            pl.BlockSpec((tm * rw, LANES), lambda i: (i, 0)),
            pl.BlockSpec(memory_space=pl.ANY),
        ],
        out_specs=pl.BlockSpec(memory_space=pl.ANY),
        out_shape=jax.ShapeDtypeStruct(init.shape, init.dtype),
        scratch_shapes=[pltpu.SemaphoreType.DMA(())],
        input_output_aliases={2: 0},
        compiler_params=pltpu.CompilerParams(dimension_semantics=("arbitrary",), has_side_effects=True),
        name="scatter_rows",
    )(pos_flat, src, init)


def _weights_changed(te_ref):
    m = pl.program_id(1)
    return (m == 0) | (te_ref[m] != te_ref[jnp.maximum(m - 1, 0)])


def _gm1_kernel(te_ref, tv_ref, xs_ref, wg_ref, wl_ref, bg_ref, bl_ref, h_ref, wgb_ref, wlb_ref):
    m = pl.program_id(1)

    @pl.when(_weights_changed(te_ref))
    def _():
        wgb_ref[...] = wg_ref[...].astype(BF16)
        wlb_ref[...] = wl_ref[...].astype(BF16)

    @pl.when(tv_ref[m] != 0)
    def _():
        x = _load_token_rows(xs_ref, h_ref.shape[0]).astype(BF16)
        tn = h_ref.shape[1]
        piece = min(tn, GM1_PIECE)
        for c0 in range(0, tn, piece):
            cs = slice(c0, c0 + piece)
            hg = jnp.dot(x, wgb_ref[:, cs], preferred_element_type=F32) + bg_ref[:, cs]
            hl = jnp.dot(x, wlb_ref[:, cs], preferred_element_type=F32) + bl_ref[:, cs]
            hg = jnp.minimum(hg, SWIGLU_LIMIT)
            hl = jnp.clip(hl, -SWIGLU_LIMIT, SWIGLU_LIMIT)
            h_ref[:, cs] = (hg * _sigmoid(SWIGLU_ALPHA * hg) * (hl + 1.0)).astype(BF16)

    @pl.when(tv_ref[m] == 0)
    def _():
        h_ref[...] = jnp.zeros_like(h_ref)


def _gm1(tile_e, tile_v, xs, w1, b1, tme, tn):
    E, D, F2 = w1.shape
    rw = D // 2 // LANES
    Np = xs.shape[0] // rw
    Fh = F2 // 2
    nj = Fh // tn
    grid_spec = pltpu.PrefetchScalarGridSpec(
        num_scalar_prefetch=2,
        grid=(nj, Np // tme),
        in_specs=[
            pl.BlockSpec((tme * rw, LANES), lambda j, m, te, tv: (m, 0)),
            pl.BlockSpec((None, D, tn), lambda j, m, te, tv: (te[m], 0, j)),
            pl.BlockSpec((None, D, tn), lambda j, m, te, tv: (te[m], 0, nj + j)),
            pl.BlockSpec((None, 1, tn), lambda j, m, te, tv: (te[m], 0, j)),
            pl.BlockSpec((None, 1, tn), lambda j, m, te, tv: (te[m], 0, nj + j)),
        ],
        out_specs=pl.BlockSpec((tme, tn), lambda j, m, te, tv: (m, j)),
        scratch_shapes=[pltpu.VMEM((D, tn), BF16), pltpu.VMEM((D, tn), BF16)],
    )
    return pl.pallas_call(
        _gm1_kernel,
        grid_spec=grid_spec,
        out_shape=jax.ShapeDtypeStruct((Np, Fh), BF16),
        compiler_params=_cparams(("arbitrary", "arbitrary")),
        name="gm1",
    )(tile_e, tile_v, xs, w1, w1, b1, b1)


def _gm2_kernel(te_ref, tv_ref, h_ref, w_ref, b_ref, y_ref, wb_ref):
    m = pl.program_id(1)
    half = w_ref.shape[1] // 2

    @pl.when(_weights_changed(te_ref))
    def _():
        wb_ref[...] = w_ref[...].astype(BF16)

    @pl.when(tv_ref[m] != 0)
    def _():
        y = jnp.dot(h_ref[...], wb_ref[...], preferred_element_type=F32) + b_ref[...]
        _store_token_rows(y_ref, y[:, :half], y[:, half:])

    @pl.when(tv_ref[m] == 0)
    def _():
        y_ref[...] = jnp.zeros_like(y_ref)


def _gm2(tile_e, tile_v, hs, w2, b2, tme):
    Np, Fh = hs.shape
    E, _, D = w2.shape
    rw = D // 2 // LANES
    grid_spec = pltpu.PrefetchScalarGridSpec(
        num_scalar_prefetch=2,
        grid=(1, Np // tme),
        in_specs=[
            pl.BlockSpec((tme, Fh), lambda j, m, te, tv: (m, 0)),
            pl.BlockSpec((None, Fh, D), lambda j, m, te, tv: (te[m], 0, 0), pipeline_mode=pl.Buffered(1)),
            pl.BlockSpec((None, 1, D), lambda j, m, te, tv: (te[m], 0, 0)),
        ],
        out_specs=pl.BlockSpec((tme * rw, LANES), lambda j, m, te, tv: (m, 0)),
        scratch_shapes=[pltpu.VMEM((Fh, D), BF16)],
    )
    return pl.pallas_call(
        _gm2_kernel,
        grid_spec=grid_spec,
        out_shape=jax.ShapeDtypeStruct((Np * rw, LANES), U32),
        compiler_params=_cparams(("arbitrary", "arbitrary")),
        name="gm2",
    )(tile_e, tile_v, hs, w2, b2)


def _final_kernel(pos_ref, base_ref, ys_ref, rt_ref, g_ref, b_ref, o_ref, y_ref, sem):
    tm = base_ref.shape[0]
    rw = y_ref.shape[1] // tm

    def issue(t, carry):
        for kk in range(TOP_K):
            _token_copy(ys_ref, pos_ref[t * TOP_K + kk], y_ref.at[kk], t, sem, rw).start(priority=kk % 2)
        return carry

    lax.fori_loop(0, tm, issue, 0)
    for kk in range(TOP_K):
        _wait_tokens(ys_ref, y_ref.at[kk], sem)

    acc = base_ref[...]
    rt = rt_ref[...]
    for kk in range(TOP_K):
        acc = acc + rt[:, TOP_K + kk:TOP_K + kk + 1] * _load_token_rows(y_ref.at[kk], tm)
    o_ref[...] = _ln(acc, g_ref[...], b_ref[...])


def _final(pos_flat, base, ys, rt, g, b, tm):
    T, D = base.shape
    rw = D // 2 // LANES
    return pl.pallas_call(
        _final_kernel,
        grid=(T // tm,),
        in_specs=[
            pl.BlockSpec((tm * TOP_K,), lambda i: (i,), memory_space=pltpu.SMEM),
            pl.BlockSpec((tm, D), lambda i: (i, 0)),
            pl.BlockSpec(memory_space=pl.ANY),
            pl.BlockSpec((tm, LANES), lambda i: (i, 0)),
            pl.BlockSpec((1, D), lambda i: (0, 0)),
            pl.BlockSpec((1, D), lambda i: (0, 0)),
        ],
        out_specs=pl.BlockSpec((tm, D), lambda i: (i, 0)),
        out_shape=jax.ShapeDtypeStruct((T, D), F32),
        scratch_shapes=[pltpu.VMEM((TOP_K, tm * rw, LANES), U32), pltpu.SemaphoreType.DMA(())],
        compiler_params=_cparams(("arbitrary",)),
        name="final",
    )(pos_flat, base, ys, rt, g, b)


def _tile_sizes(T, seq, D):
    small = seq < 512
    return dict(
        tm_in=min(1024, seq), tn_in=min(1024, 8 * D),
        tm_mix=min(256, seq), tm_route=min(256, seq),
        tme=128 if small else 512, tn_e=min(1024, D),
        tm_perm=256 if small else 512,
    )


def _encoder_layer(x, p, n_seq, seq, layer, depth, ln_emb_g, ln_emb_b, w_in, lower_bounds, hg_norm_g,
                   w_hg_out, conv_w, conv_b, conv_ln_g, conv_ln_b, w_conv_out, w_out, ln1_g, ln1_b,
                   w_router, b_router, w_exp1, b_exp1, w_exp2, b_exp2, w_ple_gate, w_ple_proj,
                   ln2_g, ln2_b):
    T, D = x.shape
    E = w_router.shape[1]
    H = D // LANES
    ts = _tile_sizes(T, seq, D)
    alpha = (2.0 * depth) ** 0.25
    row = lambda v: v.reshape(1, -1)

    P = _inproj(x, row(ln_emb_g), row(ln_emb_b), w_in.astype(BF16), ts["tm_in"], ts["tn_in"])
    lbw = lower_bounds.reshape(2, lower_bounds.shape[1], H, 1, LANES).transpose(0, 2, 1, 3, 4)
    o_f, o_b = _scan(P, lbw, n_seq, seq, layer)
    small = (row(hg_norm_g), conv_w, row(conv_b), row(conv_ln_g), row(conv_ln_b),
             row(ln_emb_g), row(ln_emb_b), row(ln1_g), row(ln1_b))
    x1 = _mix(o_f, o_b, P, x, w_hg_out.astype(BF16), w_conv_out.astype(BF16), w_out.astype(BF16),
              small, seq, ts["tm_mix"], alpha)
    base, xp, rt, counts = _route(x1, p, w_ple_gate.astype(BF16), w_ple_proj.astype(BF16),
                                  w_router, row(b_router), ts["tm_route"], alpha)

    tme = ts["tme"]
    n_tiles = (T * TOP_K) // tme + E
    cnt = counts[0].astype(I32)
    padded = ((cnt + tme - 1) // tme) * tme
    ends = jnp.cumsum(padded)
    starts = ends - padded
    idx = rt[:, :TOP_K].astype(I32)
    rank = rt[:, 2 * TOP_K:3 * TOP_K].astype(I32)
    pos = (starts[idx] + rank).reshape(-1)
    tile_start = jnp.arange(n_tiles, dtype=I32) * tme
    tile_e = jnp.minimum(jnp.sum((tile_start[:, None] >= ends[None, :]).astype(I32), axis=1), E - 1)
    tile_v = (tile_start < ends[-1]).astype(I32)

    xs = _scatter_rows(pos, xp, n_tiles * tme, ts["tm_perm"], D // 2 // LANES)
    hs = _gm1(tile_e, tile_v, xs, w_exp1, b_exp1.reshape(E, 1, -1), tme, ts["tn_e"])
    ys = _gm2(tile_e, tile_v, hs, w_exp2, b_exp2.reshape(E, 1, -1), tme)
    return _final(pos, base, ys, rt, row(ln2_g), row(ln2_b), ts["tm_perm"])


def kernel(x_prompt, x_sample, p_prompt, p_sample, ln_emb_g, ln_emb_b, w_in, lower_bounds, hg_norm_g, w_hg_out, conv_w, conv_b, conv_ln_g, conv_ln_b, w_conv_out, w_out, ln1_g, ln1_b, w_router, b_router, w_exp1, b_exp1, w_exp2, b_exp2, w_ple_gate, w_ple_proj, ln2_g, ln2_b):
    depth = w_in.shape[0]
    assert depth == 1, "the per-layer input embeddings and weights are wired for a single layer"
    D = x_prompt.shape[-1]
    seq = x_prompt.shape[1]
    assert x_sample.shape[1] == seq
    nb_p, nb_s = x_prompt.shape[0], x_sample.shape[0]
    x = jnp.concatenate([x_prompt.reshape(-1, D), x_sample.reshape(-1, D)], axis=0)
    p = jnp.concatenate([p_prompt[0].reshape(-1, p_prompt.shape[-1]),
                         p_sample[0].reshape(-1, p_sample.shape[-1])], axis=0)
    y = _encoder_layer(x, p, nb_p + nb_s, seq, 0, depth, ln_emb_g, ln_emb_b, w_in[0], lower_bounds,
                       hg_norm_g[0], w_hg_out[0], conv_w[0], conv_b[0], conv_ln_g[0], conv_ln_b[0],
                       w_conv_out[0], w_out[0], ln1_g[0], ln1_b[0], w_router[0], b_router[0],
                       w_exp1[0], b_exp1[0], w_exp2[0], b_exp2[0], w_ple_gate[0], w_ple_proj[0],
                       ln2_g[0], ln2_b[0])
    n_p = nb_p * seq
    return (y[:n_p].reshape(x_prompt.shape), y[n_p:].reshape(x_sample.shape))
```

```python
import functools
import math

import jax
import jax.numpy as jnp
from jax import lax
from jax.experimental import pallas as pl
from jax.experimental.pallas import tpu as pltpu

F32 = jnp.float32
BF16 = jnp.bfloat16
U32 = jnp.uint32
I32 = jnp.int32

LANES = 128
NORM_EPS = 1e-5
CONV_K = 31
TOP_K = 4
SWIGLU_ALPHA = 1.702
SWIGLU_LIMIT = 7.0
SCAN_CHUNK = 64
SCAN_GROUP = 4
DIAG = 8
MAX_LOG_DECAY = 80.0
GM1_PIECE = 512
VMEM_LIMIT = 56 * 1024 * 1024


def _cparams(sem, vmem=VMEM_LIMIT):
    return pltpu.CompilerParams(dimension_semantics=sem, vmem_limit_bytes=vmem)


def _resident(shape):
    nd = len(shape)
    return pl.BlockSpec(shape, lambda *_: (0,) * nd, pipeline_mode=pl.Buffered(1))


def _ln(x, g, b):
    mu = jnp.mean(x, axis=-1, keepdims=True)
    xc = x - mu
    var = jnp.mean(xc * xc, axis=-1, keepdims=True)
    return xc * lax.rsqrt(var + NORM_EPS) * g + b


def _sigmoid(x):
    return 1.0 / (1.0 + jnp.exp(-x))


def _pack_bf16_pair(hi, lo):
    hb = pltpu.bitcast(hi.astype(BF16).astype(F32), U32)
    lb = pltpu.bitcast(lo.astype(BF16).astype(F32), U32)
    return (hb & jnp.uint32(0xFFFF0000)) | (lb >> 16)


def _unpack_bf16_pair(w):
    hi = pltpu.bitcast(w & jnp.uint32(0xFFFF0000), F32)
    lo = pltpu.bitcast(w << 16, F32)
    return hi, lo


def _store_token_rows(ref, hi, lo):
    tm, W = hi.shape
    rw = W // LANES
    packed = _pack_bf16_pair(hi, lo)
    for c in range(rw):
        ref[pl.ds(c, tm, stride=rw), :] = packed[:, c * LANES:(c + 1) * LANES]


def _load_token_rows(ref, tm):
    rw = ref.shape[0] // tm
    pairs = [_unpack_bf16_pair(ref[pl.ds(c, tm, stride=rw), :]) for c in range(rw)]
    return jnp.concatenate([h for h, _ in pairs] + [l for _, l in pairs], axis=-1)


def _two_part_specs(block, n_a, **kw):
    return (pl.BlockSpec(block, lambda i, *_: (jnp.minimum(i, n_a - 1), 0), **kw),
            pl.BlockSpec(block, lambda i, *_: (jnp.maximum(i - n_a, 0), 0), **kw))


def _inproj_kernel(x_ref, g_ref, b_ref, w_ref, o_ref, h_ref):
    @pl.when(pl.program_id(1) == 0)
    def _():
        h_ref[...] = _ln(x_ref[...], g_ref[...], b_ref[...]).astype(BF16)

    acc = jnp.dot(h_ref[...], w_ref[...], preferred_element_type=F32)
    for c in range(o_ref.shape[0]):
        o_ref[c] = acc[:, c * LANES:(c + 1) * LANES].astype(BF16)


def _inproj(x, g, b, w_bf, tm, tn):
    T, D = x.shape
    N = w_bf.shape[1]
    return pl.pallas_call(
        _inproj_kernel,
        grid=(T // tm, N // tn),
        in_specs=[
            pl.BlockSpec((tm, D), lambda i, j: (i, 0)),
            pl.BlockSpec((1, D), lambda i, j: (0, 0)),
            pl.BlockSpec((1, D), lambda i, j: (0, 0)),
            pl.BlockSpec((D, tn), lambda i, j: (0, j)),
        ],
        out_specs=pl.BlockSpec((tn // LANES, tm, LANES), lambda i, j: (j, i, 0)),
        out_shape=jax.ShapeDtypeStruct((N // LANES, T, LANES), BF16),
        scratch_shapes=[pltpu.VMEM((tm, D), BF16)],
        compiler_params=_cparams(("parallel", "arbitrary")),
        name="inproj",
    )(x, g, b, w_bf)


def _cumsum_rows(g, tri):
    g1 = g.astype(BF16)
    r1 = g - g1.astype(F32)
    g2 = r1.astype(BF16)
    g3 = (r1 - g2.astype(F32)).astype(BF16)
    out = jnp.dot(tri, g1, preferred_element_type=F32)
    out = out + jnp.dot(tri, g2, preferred_element_type=F32)
    return out + jnp.dot(tri, g3, preferred_element_type=F32)


def _scan_chunk(q, v_bf, vt_bf, fl, lb, st, rev):
    C = q.shape[0]
    gate = lb + (1.0 - lb) * _sigmoid(fl)
    k = 1.0 - gate
    g = jnp.log(gate)
    row = lax.broadcasted_iota(I32, (C, C), 0)
    col = lax.broadcasted_iota(I32, (C, C), 1)
    tri = (col >= row) if rev else (col <= row)
    b = _cumsum_rows(g, jnp.where(tri, 1.0, 0.0).astype(BF16))
    b_end = b[0:1, :] if rev else b[C - 1:C, :]
    qe = (q * jnp.exp(b)).astype(BF16)

    scores = _scores_unbounded(q, k, gate, b, row, col, rev)
    kdec = (k * jnp.exp(b_end - b)).astype(BF16)
    o = jnp.dot(scores.astype(BF16), v_bf, preferred_element_type=F32)
    o = o + lax.dot_general(qe, st.astype(BF16), (((1,), (1,)), ((), ())), preferred_element_type=F32)
    st_new = st * jnp.exp(b_end) + jnp.dot(vt_bf, kdec, preferred_element_type=F32)
    return o, st_new


def _scores_unbounded(q, k, gate, b, row, col, rev):
    C = q.shape[0]
    scores = jnp.zeros((C, C), F32)
    m = C // 2
    while m >= DIAG:
        n = C // m
        bm = b.reshape(n, m, b.shape[-1])
        zero = jnp.zeros((1, 1, b.shape[-1]), F32)
        if rev:
            edge = bm[:, 0:1, :]
            ref_q = jnp.concatenate([edge[1:], zero], axis=0)
        else:
            edge = bm[:, m - 1:m, :]
            ref_q = jnp.concatenate([zero, edge[:-1]], axis=0)
        eq = (bm - ref_q).reshape(C, -1)
        ek = (edge - bm).reshape(C, -1)
        ql = (q * jnp.exp(eq)).astype(BF16)
        kl = (k * jnp.exp(ek)).astype(BF16)
        s_l = lax.dot_general(ql, kl, (((1,), (1,)), ((), ())), preferred_element_type=F32)
        rb, cb = row // m, col // m
        if rev:
            mask = (cb == rb + 1) & ((rb & 1) == 0)
        else:
            mask = (rb == cb + 1) & ((cb & 1) == 0)
        scores = scores + jnp.where(mask, s_l, 0.0)
        m //= 2

    rin = row & (DIAG - 1)
    gprod = None
    for d in range(DIAG):
        if d == 0:
            a = q * k
            mask = row == col
        else:
            sh = (C - d) if rev else d
            gprod = gate if d == 1 else gate * pltpu.roll(gprod, (C - 1) if rev else 1, axis=0)
            a = q * pltpu.roll(k, sh, axis=0) * gprod
            if rev:
                mask = (col == row + d) & (rin + d < DIAG)
            else:
                mask = (col == row - d) & (rin >= d)
        scores = scores + jnp.where(mask, jnp.sum(a, axis=-1, keepdims=True), 0.0)
    return scores


def _scan_group_masks(G, C, W, rev):
    R = G * C
    t_row = lax.broadcasted_iota(I32, (C, 3 * C), 0)
    t_col = lax.broadcasted_iota(I32, (C, 3 * C), 1) & (C - 1)
    tri3 = jnp.where((t_col >= t_row) if rev else (t_col <= t_row), 1.0, 0.0).astype(BF16)
    row = lax.broadcasted_iota(I32, (R, R), 0)
    col = lax.broadcasted_iota(I32, (R, R), 1)
    order = (col >= row) if rev else (col <= row)
    block = jnp.where(order & ((row // C) == (col // C)), 1.0, 0.0)
    q_head = lax.broadcasted_iota(I32, (R, W), 0) // C
    v_head = lax.broadcasted_iota(I32, (W, R), 1) // C
    q_sel = [jnp.where(q_head == h, 1.0, 0.0).astype(BF16) for h in range(G)]
    v_sel = [jnp.where(v_head == h, 1.0, 0.0).astype(BF16) for h in range(G)]
    return tri3, block, q_sel, v_sel


def _scan_group(q4, v4, f4, lb4, st4, rev, masks):
    G, C, W = q4.shape
    R = G * C
    stack = lambda t: t.reshape(R, W)
    qr = stack(q4.astype(F32))
    q = qr * _sigmoid(qr)
    gate = lb4 + (1.0 - lb4) * _sigmoid(f4.astype(F32))
    k = stack(1.0 - gate)
    g = jnp.log(gate)

    g1 = g.astype(BF16)
    r1 = g - g1.astype(F32)
    g2 = r1.astype(BF16)
    g3 = (r1 - g2.astype(F32)).astype(BF16)
    wide = lambda t: jnp.concatenate([t[h] for h in range(G)], axis=-1)
    g_split = jnp.concatenate([wide(g1), wide(g2), wide(g3)], axis=0)
    tri3, block, q_sel, v_sel = masks
    b_wide = jnp.dot(tri3, g_split, preferred_element_type=F32)
    b = jnp.concatenate([b_wide[:, h * W:(h + 1) * W] for h in range(G)], axis=0)

    qe = (q * jnp.exp(b)).astype(BF16)
    kt = k * jnp.exp(-b)
    s_all = lax.dot_general(qe, kt.astype(BF16), (((1,), (1,)), ((), ())), preferred_element_type=F32)
    scores = jnp.where(block > 0.5, s_all, 0.0).astype(BF16)
    o = jnp.dot(scores, stack(v4), preferred_element_type=F32)

    q_blk = jnp.concatenate([qe * q_sel[h] for h in range(G)], axis=-1)
    st_cat = jnp.concatenate([st4[h].astype(BF16) for h in range(G)], axis=-1)
    o = o + lax.dot_general(q_blk, st_cat, (((1,), (1,)), ((), ())), preferred_element_type=F32)

    b3 = b.reshape(G, C, W)
    dec = jnp.exp(b3[:, 0:1, :] if rev else b3[:, C - 1:C, :])
    kdec = stack(kt.reshape(G, C, W) * dec).astype(BF16)
    vt = stack(v4.astype(F32)).T.astype(BF16)
    vt_blk = jnp.concatenate([vt * v_sel[h] for h in range(G)], axis=0)
    upd = jnp.dot(vt_blk, kdec, preferred_element_type=F32)
    return o.reshape(G, C, W), st4 * dec + upd.reshape(G, W, W)


def _scan_kernel(qf_ref, vf_ref, ff_ref, qb_ref, vb_ref, fb_ref, lbw_ref, of_ref, ob_ref, sf_ref, sb_ref,
                 *, layer):
    @pl.when(pl.program_id(1) == 0)
    def _():
        sf_ref[...] = jnp.zeros_like(sf_ref)
        sb_ref[...] = jnp.zeros_like(sb_ref)

    def lower_bounds(raw):
        e = jnp.exp(raw - jnp.max(raw, axis=-3, keepdims=True))
        return jnp.sum(e[..., :layer + 1, :, :], axis=-3) / jnp.sum(e, axis=-3)

    dirs = ((qf_ref, vf_ref, ff_ref, of_ref, sf_ref), (qb_ref, vb_ref, fb_ref, ob_ref, sb_ref))
    H = qf_ref.shape[0]
    G = math.gcd(H, SCAN_GROUP)

    def run_bounded():
        masks = [_scan_group_masks(G, qf_ref.shape[1], qf_ref.shape[2], rev) for rev in (False, True)]

        def body(gi, carry):
            hs = pl.ds(pl.multiple_of(gi * G, G), G)
            for rev, (q_ref, v_ref, f_ref, o_ref, s_ref) in enumerate(dirs):
                o, st = _scan_group(q_ref[hs], v_ref[hs], f_ref[hs], lower_bounds(lbw_ref[rev, hs]),
                                    s_ref[hs], bool(rev), masks[rev])
                o_ref[hs] = o.astype(BF16)
                s_ref[hs] = st
            return carry

        lax.fori_loop(0, H // G, body, 0, unroll=2)

    def run_unbounded():
        def body(h, carry):
            for rev, (q_ref, v_ref, f_ref, o_ref, s_ref) in enumerate(dirs):
                qr = q_ref[h].astype(F32)
                q = qr * _sigmoid(qr)
                v = v_ref[h].astype(F32)
                o, st = _scan_chunk(q, v.astype(BF16), v.T.astype(BF16), f_ref[h].astype(F32),
                                    lower_bounds(lbw_ref[rev, h]), s_ref[h], bool(rev))
                o_ref[h] = o.astype(BF16)
                s_ref[h] = st
            return carry

        lax.fori_loop(0, H, body, 0)

    lb_all = lower_bounds(lbw_ref[...])
    lb_min = jnp.min(jnp.min(lb_all, axis=(0, 1)))
    is_bounded = lb_min > math.exp(-MAX_LOG_DECAY / SCAN_CHUNK)
    pl.when(is_bounded)(run_bounded)
    pl.when(jnp.logical_not(is_bounded))(run_unbounded)


def _scan(P, lbw, n_seq, seq, layer):
    H = lbw.shape[1]
    T = P.shape[1]
    C = SCAN_CHUNK
    nc = seq // C

    def spec(sec, rev):
        if rev:
            return pl.BlockSpec((H, C, LANES), lambda s, c: (sec, s * nc + nc - 1 - c, 0))
        return pl.BlockSpec((H, C, LANES), lambda s, c: (sec, s * nc + c, 0))

    out_sds = jax.ShapeDtypeStruct((H, T, LANES), BF16)
    return pl.pallas_call(
        functools.partial(_scan_kernel, layer=layer),
        grid=(n_seq, nc),
        in_specs=[spec(0, False), spec(3, False), spec(1, False),
                  spec(0, True), spec(3, True), spec(2, True),
                  pl.BlockSpec(lbw.shape, lambda s, c: (0,) * lbw.ndim)],
        out_specs=[spec(0, False), spec(0, True)],
        out_shape=[out_sds, out_sds],
        scratch_shapes=[pltpu.VMEM((H, LANES, LANES), F32), pltpu.VMEM((H, LANES, LANES), F32)],
        compiler_params=_cparams(("parallel", "arbitrary")),
        name="scan",
    )(P, P, P, P, P, P, lbw)


def _mix_kernel(of_ref, ob_ref, og_ref, glu_ref, gprev_ref, gnext_ref, ga_ref, gc_ref, x_ref,
                whg_ref, wcv_ref, wout_ref, hgn_ref, cw_ref, cb_ref, clg_ref, clb_ref,
                eg_ref, eb_ref, l1g_ref, l1b_ref, x1_ref, hc_ref, cv_ref, sh_ref, *, tiles_per_seq, alpha):
    H, tm, _ = of_ref.shape
    nch = glu_ref.shape[0] // 2
    halo = gprev_ref.shape[1]
    i = pl.program_id(0)
    first = (i % tiles_per_seq) == 0
    last = (i % tiles_per_seq) == tiles_per_seq - 1

    def cat(ref, lo=0, hi=None):
        hi = ref.shape[0] if hi is None else hi
        return jnp.concatenate([ref[c].astype(F32) for c in range(lo, hi)], axis=-1)

    o = cat(of_ref) + cat(ob_ref)
    og = cat(og_ref)
    on = o * lax.rsqrt(jnp.mean(o * o, axis=-1, keepdims=True) + NORM_EPS) * hgn_ref[...]
    a = jnp.dot((on * (og * _sigmoid(og))).astype(BF16), whg_ref[...], preferred_element_type=F32)

    def glu(ref):
        return cat(ref, 0, nch) * _sigmoid(cat(ref, nch, 2 * nch))

    hc_ref[pl.ds(0, halo), :] = jnp.where(first, 0.0, glu(gprev_ref))
    hc_ref[pl.ds(halo, tm), :] = glu(glu_ref)
    hc_ref[pl.ds(halo + tm, halo), :] = jnp.where(last, 0.0, glu(gnext_ref))
    rb = 32
    sub = 8
    span = sh_ref.shape[1]
    for c in range(nch):
        cs = slice(c * LANES, (c + 1) * LANES)
        taps = [cw_ref[j:j + 1, cs] for j in range(CONV_K)]
        for s in range(1, sub):
            sh_ref[s] = hc_ref[s:s + span, cs]
        for r0 in range(0, tm, rb):
            acc = jnp.zeros((rb, LANES), F32)
            for j in range(CONV_K):
                whole, phase = divmod(halo - CONV_K // 2 + j, sub)
                lo = r0 + sub * whole
                win = hc_ref[lo:lo + rb, cs] if phase == 0 else sh_ref[phase, lo:lo + rb, :]
                acc = acc + taps[j] * win
            cv_ref[r0:r0 + rb, cs] = acc
    cn = _ln(cv_ref[...] + cb_ref[...], clg_ref[...], clb_ref[...])
    cc = jnp.dot((cn * _sigmoid(cn)).astype(BF16), wcv_ref[...], preferred_element_type=F32)

    mixed = jnp.dot((_sigmoid(cat(ga_ref)) * a + _sigmoid(cat(gc_ref)) * cc).astype(BF16), wout_ref[...],
                    preferred_element_type=F32)
    h = _ln(x_ref[...], eg_ref[...], eb_ref[...])
    x1_ref[...] = _ln(alpha * h + mixed, l1g_ref[...], l1b_ref[...])


def _mix(o_f, o_b, P, x, whg, wcv, wout, small, seq, tm, alpha):
    H, T, _ = o_f.shape
    D = x.shape[1]
    Cw = wcv.shape[0]
    halo = 16
    nt = T // tm
    hb = tm // halo
    last_hb = T // halo - 1
    sec = lambda s: pl.BlockSpec((H, tm, LANES), lambda i: (s, i, 0))
    vec = lambda n: pl.BlockSpec((1, n), lambda i: (0, 0))
    in_specs = [
        sec(0), sec(0), sec(4), sec(5),
        pl.BlockSpec((H, halo, LANES), lambda i: (5, jnp.maximum(i * hb - 1, 0), 0)),
        pl.BlockSpec((H, halo, LANES), lambda i: (5, jnp.minimum((i + 1) * hb, last_hb), 0)),
        sec(6), sec(7),
        pl.BlockSpec((tm, D), lambda i: (i, 0)),
        _resident(whg.shape), _resident(wcv.shape), _resident(wout.shape),
        vec(D), pl.BlockSpec((CONV_K, Cw), lambda i: (0, 0)), vec(Cw), vec(Cw), vec(Cw),
        vec(D), vec(D), vec(D), vec(D),
    ]
    return pl.pallas_call(
        functools.partial(_mix_kernel, tiles_per_seq=seq // tm, alpha=alpha),
        grid=(nt,),
        in_specs=in_specs,
        out_specs=pl.BlockSpec((tm, D), lambda i: (i, 0)),
        out_shape=jax.ShapeDtypeStruct((T, D), F32),
        scratch_shapes=[pltpu.VMEM((tm + 2 * halo, Cw), F32), pltpu.VMEM((tm, Cw), F32),
                        pltpu.VMEM((8, tm + 2 * halo - 8, LANES), F32)],
        compiler_params=_cparams(("parallel",)),
        name="mix",
    )(o_f, o_b, P, P, P, P, P, P, x, whg, wcv, wout, *small)


def _route_kernel(x1_ref, p_ref, wpg_ref, wpp_ref, wr_ref, br_ref, base_ref, xp_ref, rt_ref, cnt_ref,
                  *, alpha):
    tm, D = x1_ref.shape
    E = wr_ref.shape[1]

    @pl.when(pl.program_id(0) == 0)
    def _():
        cnt_ref[...] = jnp.zeros_like(cnt_ref)

    x1 = x1_ref[...]
    xb = x1.astype(BF16)
    gate = _sigmoid(jnp.dot(xb, wpg_ref[...], preferred_element_type=F32))
    ple = gate * jnp.dot(p_ref[...].astype(BF16), wpp_ref[...], preferred_element_type=F32)
    base_ref[...] = alpha * x1 + ple
    _store_token_rows(xp_ref, x1[:, :D // 2], x1[:, D // 2:])

    wr = wr_ref[...]
    wr_hi = wr.astype(BF16)
    wr_lo = (wr - wr_hi.astype(F32)).astype(BF16)
    x_lo = (x1 - xb.astype(F32)).astype(BF16)
    logits = (jnp.dot(xb, wr_hi, preferred_element_type=F32)
              + jnp.dot(xb, wr_lo, preferred_element_type=F32)
              + jnp.dot(x_lo, wr_hi, preferred_element_type=F32)) + br_ref[...]
    col = lax.broadcasted_iota(I32, (tm, E), 1).astype(F32)
    lane = lax.broadcasted_iota(I32, (tm, LANES), 1)
    sels, vals = [], []
    work = logits
    for _ in range(TOP_K):
        mx = jnp.max(work, axis=-1, keepdims=True)
        idx = jnp.min(jnp.where(work == mx, col, float(E)), axis=-1, keepdims=True)
        sel = col == idx
        sels.append((sel, idx))
        vals.append(mx)
        work = jnp.where(sel, -jnp.inf, work)
    pf = sum(jnp.where(sel, 1.0, 0.0) for sel, _ in sels)
    r = lax.broadcasted_iota(I32, (tm, tm), 0)
    c = lax.broadcasted_iota(I32, (tm, tm), 1)
    before = jnp.dot(jnp.where(c < r, 1.0, 0.0).astype(BF16), pf.astype(BF16),
                     preferred_element_type=F32) + cnt_ref[...]
    es = [jnp.exp(v - vals[0]) for v in vals]
    den = es[0] + es[1] + es[2] + es[3]
    out = jnp.zeros((tm, LANES), F32)
    for kk in range(TOP_K):
        sel, idx = sels[kk]
        rank = jnp.sum(jnp.where(sel, before, 0.0), axis=-1, keepdims=True)
        out = jnp.where(lane == kk, idx, out)
        out = jnp.where(lane == TOP_K + kk, es[kk] / den, out)
        out = jnp.where(lane == 2 * TOP_K + kk, rank, out)
    rt_ref[...] = out
    cnt_ref[...] = cnt_ref[...] + jnp.sum(pf, axis=0, keepdims=True)


def _route(x1, p, wpg, wpp, wr, br, tm, alpha):
    T, D = x1.shape
    E = wr.shape[1]
    return pl.pallas_call(
        functools.partial(_route_kernel, alpha=alpha),
        grid=(T // tm,),
        in_specs=[
            pl.BlockSpec((tm, D), lambda i: (i, 0)),
            pl.BlockSpec((tm, p.shape[1]), lambda i: (i, 0)),
            _resident(wpg.shape), _resident(wpp.shape),
            pl.BlockSpec(wr.shape, lambda i: (0, 0)),
            pl.BlockSpec((1, E), lambda i: (0, 0)),
        ],
        out_specs=[
            pl.BlockSpec((tm, D), lambda i: (i, 0)),
            pl.BlockSpec((tm * (D // 2) // LANES, LANES), lambda i: (i, 0)),
            pl.BlockSpec((tm, LANES), lambda i: (i, 0)),
            pl.BlockSpec((1, E), lambda i: (0, 0)),
        ],
        out_shape=[
            jax.ShapeDtypeStruct((T, D), F32),
            jax.ShapeDtypeStruct((T * (D // 2) // LANES, LANES), U32),
            jax.ShapeDtypeStruct((T, LANES), F32),
            jax.ShapeDtypeStruct((1, E), F32),
        ],
        compiler_params=_cparams(("arbitrary",)),
        name="route",
    )(x1, p, wpg, wpp, wr, br)


def _token_copy(src, s, dst, d, sem, rw):
    return pltpu.make_async_copy(src.at[pl.ds(pl.multiple_of(s * rw, rw), rw), :],
                                 dst.at[pl.ds(pl.multiple_of(d * rw, rw), rw), :], sem)


def _wait_tokens(hbm, vmem, sem):
    pltpu.make_async_copy(hbm.at[pl.ds(0, vmem.shape[0]), :], vmem, sem).wait()


def _scatter_kernel(pos_ref, src_ref, init_ref, dst_ref, sem, *, rw):
    del init_ref
    tm = src_ref.shape[0] // rw

    def issue(t, carry):
        for kk in range(TOP_K):
            _token_copy(src_ref, t, dst_ref, pos_ref[t * TOP_K + kk], sem, rw).start(priority=kk % 2)
        return carry

    lax.fori_loop(0, tm, issue, 0)
    for _ in range(TOP_K):
        _wait_tokens(dst_ref, src_ref, sem)


def _scatter_rows(pos_flat, src, n_dst, tm, rw):
    init = jnp.zeros((n_dst * rw, LANES), src.dtype)
    return pl.pallas_call(
        functools.partial(_scatter_kernel, rw=rw),
        grid=(src.shape[0] // (tm * rw),),
        in_specs=[
            pl.BlockSpec((tm * TOP_K,), lambda i: (i,), memory_space=pltpu.SMEM),
            pl.BlockSpec((tm * rw, LANES), lambda i: (i, 0)),
            pl.BlockSpec(memory_space=pl.ANY),
        ],
        out_specs=pl.BlockSpec(memory_space=pl.ANY),
        out_shape=jax.ShapeDtypeStruct(init.shape, init.dtype),
        scratch_shapes=[pltpu.SemaphoreType.DMA(())],
        input_output_aliases={2: 0},
        compiler_params=pltpu.CompilerParams(dimension_semantics=("arbitrary",), has_side_effects=True),
        name="scatter_rows",
    )(pos_flat, src, init)


def _weights_changed(te_ref):
    m = pl.program_id(1)
    return (m == 0) | (te_ref[m] != te_ref[jnp.maximum(m - 1, 0)])


def _gm1_kernel(te_ref, tv_ref, xs_ref, wg_ref, wl_ref, bg_ref, bl_ref, h_ref, wgb_ref, wlb_ref):
    m = pl.program_id(1)

    @pl.when(_weights_changed(te_ref))
    def _():
        wgb_ref[...] = wg_ref[...].astype(BF16)
        wlb_ref[...] = wl_ref[...].astype(BF16)

    @pl.when(tv_ref[m] != 0)
    def _():
        x = _load_token_rows(xs_ref, h_ref.shape[0]).astype(BF16)
        tn = h_ref.shape[1]
        piece = min(tn, GM1_PIECE)
        for c0 in range(0, tn, piece):
            cs = slice(c0, c0 + piece)
            hg = jnp.dot(x, wgb_ref[:, cs], preferred_element_type=F32) + bg_ref[:, cs]
            hl = jnp.dot(x, wlb_ref[:, cs], preferred_element_type=F32) + bl_ref[:, cs]
            hg = jnp.minimum(hg, SWIGLU_LIMIT)
            hl = jnp.clip(hl, -SWIGLU_LIMIT, SWIGLU_LIMIT)
            h_ref[:, cs] = (hg * _sigmoid(SWIGLU_ALPHA * hg) * (hl + 1.0)).astype(BF16)

    @pl.when(tv_ref[m] == 0)
    def _():
        h_ref[...] = jnp.zeros_like(h_ref)


def _gm1(tile_e, tile_v, xs, w1, b1, tme, tn):
    E, D, F2 = w1.shape
    rw = D // 2 // LANES
    Np = xs.shape[0] // rw
    Fh = F2 // 2
    nj = Fh // tn
    grid_spec = pltpu.PrefetchScalarGridSpec(
        num_scalar_prefetch=2,
        grid=(nj, Np // tme),
        in_specs=[
            pl.BlockSpec((tme * rw, LANES), lambda j, m, te, tv: (m, 0)),
            pl.BlockSpec((None, D, tn), lambda j, m, te, tv: (te[m], 0, j)),
            pl.BlockSpec((None, D, tn), lambda j, m, te, tv: (te[m], 0, nj + j)),
            pl.BlockSpec((None, 1, tn), lambda j, m, te, tv: (te[m], 0, j)),
            pl.BlockSpec((None, 1, tn), lambda j, m, te, tv: (te[m], 0, nj + j)),
        ],
        out_specs=pl.BlockSpec((tme, tn), lambda j, m, te, tv: (m, j)),
        scratch_shapes=[pltpu.VMEM((D, tn), BF16), pltpu.VMEM((D, tn), BF16)],
    )
    return pl.pallas_call(
        _gm1_kernel,
        grid_spec=grid_spec,
        out_shape=jax.ShapeDtypeStruct((Np, Fh), BF16),
        compiler_params=_cparams(("arbitrary", "arbitrary")),
        name="gm1",
    )(tile_e, tile_v, xs, w1, w1, b1, b1)


def _gm2_kernel(te_ref, tv_ref, h_ref, w_ref, b_ref, y_ref, wb_ref):
    m = pl.program_id(1)
    half = w_ref.shape[1] // 2

    @pl.when(_weights_changed(te_ref))
    def _():
        wb_ref[...] = w_ref[...].astype(BF16)

    @pl.when(tv_ref[m] != 0)
    def _():
        y = jnp.dot(h_ref[...], wb_ref[...], preferred_element_type=F32) + b_ref[...]
        _store_token_rows(y_ref, y[:, :half], y[:, half:])

    @pl.when(tv_ref[m] == 0)
    def _():
        y_ref[...] = jnp.zeros_like(y_ref)


def _gm2(tile_e, tile_v, hs, w2, b2, tme):
    Np, Fh = hs.shape
    E, _, D = w2.shape
    rw = D // 2 // LANES
    grid_spec = pltpu.PrefetchScalarGridSpec(
        num_scalar_prefetch=2,
        grid=(1, Np // tme),
        in_specs=[
            pl.BlockSpec((tme, Fh), lambda j, m, te, tv: (m, 0)),
            pl.BlockSpec((None, Fh, D), lambda j, m, te, tv: (te[m], 0, 0), pipeline_mode=pl.Buffered(1)),
            pl.BlockSpec((None, 1, D), lambda j, m, te, tv: (te[m], 0, 0)),
        ],
        out_specs=pl.BlockSpec((tme * rw, LANES), lambda j, m, te, tv: (m, 0)),
        scratch_shapes=[pltpu.VMEM((Fh, D), BF16)],
    )
    return pl.pallas_call(
        _gm2_kernel,
        grid_spec=grid_spec,
        out_shape=jax.ShapeDtypeStruct((Np * rw, LANES), U32),
        compiler_params=_cparams(("arbitrary", "arbitrary")),
        name="gm2",
    )(tile_e, tile_v, hs, w2, b2)


def _final_kernel(pos_ref, base_ref, ys_ref, rt_ref, g_ref, b_ref, oa_ref, ob_ref, y_ref, sem, *, n_a):
    tm = base_ref.shape[0]
    i = pl.program_id(0)
    rw = y_ref.shape[1] // tm

    def issue(t, carry):
        for kk in range(TOP_K):
            _token_copy(ys_ref, pos_ref[t * TOP_K + kk], y_ref.at[kk], t, sem, rw).start(priority=kk % 2)
        return carry

    lax.fori_loop(0, tm, issue, 0)
    for kk in range(TOP_K):
        _wait_tokens(ys_ref, y_ref.at[kk], sem)

    acc = base_ref[...]
    rt = rt_ref[...]
    for kk in range(TOP_K):
        acc = acc + rt[:, TOP_K + kk:TOP_K + kk + 1] * _load_token_rows(y_ref.at[kk], tm)
    out = _ln(acc, g_ref[...], b_ref[...])

    @pl.when(i < n_a)
    def _():
        oa_ref[...] = out

    @pl.when(i >= n_a)
    def _():
        ob_ref[...] = out


def _final(pos_flat, base, ys, rt, g, b, tm, t_a):
    T, D = base.shape
    rw = D // 2 // LANES
    n_a = t_a // tm
    return pl.pallas_call(
        functools.partial(_final_kernel, n_a=n_a),
        grid=(T // tm,),
        in_specs=[
            pl.BlockSpec((tm * TOP_K,), lambda i: (i,), memory_space=pltpu.SMEM),
            pl.BlockSpec((tm, D), lambda i: (i, 0)),
            pl.BlockSpec(memory_space=pl.ANY),
            pl.BlockSpec((tm, LANES), lambda i: (i, 0)),
            pl.BlockSpec((1, D), lambda i: (0, 0)),
            pl.BlockSpec((1, D), lambda i: (0, 0)),
        ],
        out_specs=list(_two_part_specs((tm, D), n_a)),
        out_shape=[jax.ShapeDtypeStruct((t_a, D), F32), jax.ShapeDtypeStruct((T - t_a, D), F32)],
        scratch_shapes=[pltpu.VMEM((TOP_K, tm * rw, LANES), U32), pltpu.SemaphoreType.DMA(())],
        compiler_params=_cparams(("arbitrary",)),
        name="final",
    )(pos_flat, base, ys, rt, g, b)


def _tile_sizes(T, seq, D):
    small = seq < 512
    return dict(
        tm_in=min(1024, seq), tn_in=min(1024, 8 * D),
        tm_mix=min(256, seq), tm_route=min(256, seq),
        tme=128 if small else 512, tn_e=min(1024, D),
        tm_perm=256 if small else 512,
    )


def _encoder_layer(x, t_a, p, n_seq, seq, layer, depth, ln_emb_g, ln_emb_b, w_in, lower_bounds, hg_norm_g,
                   w_hg_out, conv_w, conv_b, conv_ln_g, conv_ln_b, w_conv_out, w_out, ln1_g, ln1_b,
                   w_router, b_router, w_exp1, b_exp1, w_exp2, b_exp2, w_ple_gate, w_ple_proj,
                   ln2_g, ln2_b):
    T, D = x.shape
    E = w_router.shape[1]
    H = D // LANES
    ts = _tile_sizes(T, seq, D)
    alpha = (2.0 * depth) ** 0.25
    row = lambda v: v.reshape(1, -1)

    P = _inproj(x, row(ln_emb_g), row(ln_emb_b), w_in.astype(BF16), ts["tm_in"], ts["tn_in"])
    lbw = lower_bounds.reshape(2, lower_bounds.shape[1], H, 1, LANES).transpose(0, 2, 1, 3, 4)
    o_f, o_b = _scan(P, lbw, n_seq, seq, layer)
    small = (row(hg_norm_g), conv_w, row(conv_b), row(conv_ln_g), row(conv_ln_b),
             row(ln_emb_g), row(ln_emb_b), row(ln1_g), row(ln1_b))
    x1 = _mix(o_f, o_b, P, x, w_hg_out.astype(BF16), w_conv_out.astype(BF16), w_out.astype(BF16),
              small, seq, ts["tm_mix"], alpha)
    base, xp, rt, counts = _route(x1, p, w_ple_gate.astype(BF16), w_ple_proj.astype(BF16),
                                  w_router, row(b_router), ts["tm_route"], alpha)

    tme = ts["tme"]
    n_tiles = (T * TOP_K) // tme + E
    cnt = counts[0].astype(I32)
    padded = ((cnt + tme - 1) // tme) * tme
    ends = jnp.cumsum(padded)
    starts = ends - padded
    idx = rt[:, :TOP_K].astype(I32)
    rank = rt[:, 2 * TOP_K:3 * TOP_K].astype(I32)
    pos = (starts[idx] + rank).reshape(-1)
    tile_start = jnp.arange(n_tiles, dtype=I32) * tme
    tile_e = jnp.minimum(jnp.sum((tile_start[:, None] >= ends[None, :]).astype(I32), axis=1), E - 1)
    tile_v = (tile_start < ends[-1]).astype(I32)

    xs = _scatter_rows(pos, xp, n_tiles * tme, ts["tm_perm"], D // 2 // LANES)
    hs = _gm1(tile_e, tile_v, xs, w_exp1, b_exp1.reshape(E, 1, -1), tme, ts["tn_e"])
    ys = _gm2(tile_e, tile_v, hs, w_exp2, b_exp2.reshape(E, 1, -1), tme)
    return _final(pos, base, ys, rt, row(ln2_g), row(ln2_b), ts["tm_perm"], t_a)


def kernel(x_prompt, x_sample, p_prompt, p_sample, ln_emb_g, ln_emb_b, w_in, lower_bounds, hg_norm_g, w_hg_out, conv_w, conv_b, conv_ln_g, conv_ln_b, w_conv_out, w_out, ln1_g, ln1_b, w_router, b_router, w_exp1, b_exp1, w_exp2, b_exp2, w_ple_gate, w_ple_proj, ln2_g, ln2_b):
    depth = w_in.shape[0]
    assert depth == 1, "the per-layer input embeddings and weights are wired for a single layer"
    D = x_prompt.shape[-1]
    seq = x_prompt.shape[1]
    assert x_sample.shape[1] == seq
    nb_p, nb_s = x_prompt.shape[0], x_sample.shape[0]
    x = jnp.concatenate([x_prompt.reshape(-1, D), x_sample.reshape(-1, D)], axis=0)
    p = jnp.concatenate([p_prompt[0].reshape(-1, p_prompt.shape[-1]),
                         p_sample[0].reshape(-1, p_sample.shape[-1])], axis=0)
    ya, yb = _encoder_layer(x, nb_p * seq, p, nb_p + nb_s, seq, 0, depth, ln_emb_g, ln_emb_b, w_in[0], lower_bounds,
                       hg_norm_g[0], w_hg_out[0], conv_w[0], conv_b[0], conv_ln_g[0], conv_ln_b[0],
                       w_conv_out[0], w_out[0], ln1_g[0], ln1_b[0], w_router[0], b_router[0],
                       w_exp1[0], b_exp1[0], w_exp2[0], b_exp2[0], w_ple_gate[0], w_ple_proj[0],
                       ln2_g[0], ln2_b[0])
    return (ya.reshape(x_prompt.shape), yb.reshape(x_sample.shape))
```
